```python
import math
import jax, jax.numpy as jnp
from jax import lax
import numpy as np

D_MODEL = 1024
BATCH = 4
SEQ = 8192
DEPTH = 1

N_HEADS = 8
HEAD_DIM = 64
V_HEAD_DIM = 2 * HEAD_DIM
QK_WIDTH = N_HEADS * 2 * HEAD_DIM
V_WIDTH = N_HEADS * V_HEAD_DIM
ROPE_THETA = 500000.0
ROT_DIM = HEAD_DIM // 4
Q_BLOCK = 128
CONV_CHANNELS = D_MODEL
CONV_WIDTH = 31
D_FF = 2816
NORM_EPS = 1e-5
NEG_INF = -1e30
OFF_K = QK_WIDTH
OFF_V = OFF_K + QK_WIDTH
OFF_U = OFF_V + V_WIDTH
OFF_G = OFF_U + 2 * CONV_CHANNELS
IN_WIDTH = OFF_G + 2 * D_MODEL

kernel_name = "hybrid_diffattn_conformer_macaron"


def lambda_init(layer_idx):
    return 0.8 - 0.6 * math.exp(-0.3 * layer_idx)


def rmsnorm(x, gain):
    xf = x.astype(jnp.float32)
    y = xf * lax.rsqrt(jnp.mean(xf * xf, axis=-1, keepdims=True) + NORM_EPS)
    return (y * gain.astype(jnp.float32)).astype(x.dtype)


def layernorm(x, gain, bias):
    xf = x.astype(jnp.float32)
    mu = jnp.mean(xf, axis=-1, keepdims=True)
    var = jnp.mean(jnp.square(xf - mu), axis=-1, keepdims=True)
    y = (xf - mu) * lax.rsqrt(var + NORM_EPS)
    return (y * gain.astype(jnp.float32) + bias.astype(jnp.float32)).astype(x.dtype)


def swiglu(x, w_gate_up, w_down):
    a, b = jnp.split(x @ w_gate_up, 2, axis=-1)
    return (jax.nn.silu(a) * b) @ w_down


def rope_tables(seq_len):
    pos = jnp.arange(seq_len, dtype=jnp.float32)
    inv_freq = ROPE_THETA ** (-jnp.arange(0, ROT_DIM, 2, dtype=jnp.float32) / ROT_DIM)
    ang = pos[:, None] * inv_freq[None, :]
    return jnp.cos(ang), jnp.sin(ang)


def partial_rope(x, cos, sin):
    xr = x[..., :ROT_DIM].astype(jnp.float32)
    x1, x2 = xr[..., :ROT_DIM // 2], xr[..., ROT_DIM // 2:]
    c, s = cos[None, :, None, :], sin[None, :, None, :]
    rot = jnp.concatenate([x1 * c - x2 * s, x2 * c + x1 * s], axis=-1).astype(x.dtype)
    return jnp.concatenate([rot, x[..., ROT_DIM:]], axis=-1)


def diff_attention(q, k, v, lam):
    B, S = q.shape[0], q.shape[1]
    nb = S // Q_BLOCK
    scale = HEAD_DIM ** -0.5
    q_blocks = q.reshape(B, nb, Q_BLOCK, 2 * N_HEADS, HEAD_DIM).transpose(1, 0, 2, 3, 4)
    key_pos = jnp.arange(S)

    def one_block(args):
        qb, bi = args
        s = jnp.einsum('bqhd,bkhd->bhqk', qb, k, preferred_element_type=jnp.float32) * scale
        q_pos = bi * Q_BLOCK + jnp.arange(Q_BLOCK)
        s = jnp.where(key_pos[None, :] <= q_pos[:, None], s, NEG_INF)
        p = jax.nn.softmax(s, axis=-1).reshape(B, N_HEADS, 2, Q_BLOCK, S)
        a = p[:, :, 0] - lam * p[:, :, 1]
        return jnp.einsum('bhqk,bkhe->bqhe', a.astype(v.dtype), v)

    o = lax.map(one_block, (q_blocks, jnp.arange(nb)))
    return o.transpose(1, 0, 2, 3, 4).reshape(B, S, N_HEADS, V_HEAD_DIM)


def causal_depthwise_conv(u, w, b):
    out = lax.conv_general_dilated(
        u, w.reshape(CONV_WIDTH, 1, CONV_CHANNELS).astype(u.dtype),
        window_strides=(1,), padding=((CONV_WIDTH - 1, 0),),
        dimension_numbers=('NWC', 'WIO', 'NWC'),
        feature_group_count=CONV_CHANNELS)
    return out + b


def setup_inputs(seed: int = 0) -> dict:
    key = jax.random.key(seed)
    ks = jax.random.split(key, 24)
    f32 = jnp.float32

    def nrm(k, shape, scale):
        return jax.random.normal(k, shape, f32) * scale

    def gain(k, shape):
        return 1.0 + 0.02 * jax.random.normal(k, shape, f32)

    L, D = DEPTH, D_MODEL
    return {
        "x": nrm(ks[0], (BATCH, SEQ, D), 1.0),
        "ffn1_norm": gain(ks[1], (L, D)),
        "ffn1_w_gate_up": nrm(ks[2], (L, D, 2 * D_FF), D ** -0.5),
        "ffn1_w_down": nrm(ks[3], (L, D_FF, D), D_FF ** -0.5),
        "mix_norm": gain(ks[4], (L, D)),
        "w_in": nrm(ks[5], (L, D, IN_WIDTH), D ** -0.5),
        "b_gate": nrm(ks[6], (L, 2 * D), 0.01),
        "lambda_q1": nrm(ks[7], (L, HEAD_DIM), 0.1),
        "lambda_k1": nrm(ks[8], (L, HEAD_DIM), 0.1),
        "lambda_q2": nrm(ks[9], (L, HEAD_DIM), 0.1),
        "lambda_k2": nrm(ks[10], (L, HEAD_DIM), 0.1),
        "attn_subln": gain(ks[11], (L, V_HEAD_DIM)),
        "w_attn_out": nrm(ks[12], (L, V_WIDTH, D), V_WIDTH ** -0.5),
        "conv_w": nrm(ks[13], (L, CONV_WIDTH, CONV_CHANNELS), CONV_WIDTH ** -0.5),
        "conv_b": nrm(ks[14], (L, CONV_CHANNELS), 0.01),
        "conv_ln_g": gain(ks[15], (L, CONV_CHANNELS)),
        "conv_ln_b": nrm(ks[16], (L, CONV_CHANNELS), 0.01),
        "w_conv_out": nrm(ks[17], (L, CONV_CHANNELS, D), CONV_CHANNELS ** -0.5),
        "w_out": nrm(ks[18], (L, D, D), D ** -0.5),
        "ffn2_norm": gain(ks[19], (L, D)),
        "ffn2_w_gate_up": nrm(ks[20], (L, D, 2 * D_FF), D ** -0.5),
        "ffn2_w_down": nrm(ks[21], (L, D_FF, D), D_FF ** -0.5),
        "final_norm": gain(ks[22], (D,)),
    }


def reference(x, ffn1_norm, ffn1_w_gate_up, ffn1_w_down, mix_norm, w_in, b_gate,
              lambda_q1, lambda_k1, lambda_q2, lambda_k2, attn_subln, w_attn_out,
              conv_w, conv_b, conv_ln_g, conv_ln_b, w_conv_out, w_out,
              ffn2_norm, ffn2_w_gate_up, ffn2_w_down, final_norm):
    B, S = x.shape[0], x.shape[1]
    cos, sin = rope_tables(S)
    for l in range(DEPTH):
        lam_init = lambda_init(l)
        x = x + 0.5 * swiglu(rmsnorm(x, ffn1_norm[l]), ffn1_w_gate_up[l], ffn1_w_down[l])

        h = rmsnorm(x, mix_norm[l])
        z = h @ w_in[l]
        q, k, v, u, g = jnp.split(z, [OFF_K, OFF_V, OFF_U, OFF_G], axis=-1)

        q = partial_rope(q.reshape(B, S, 2 * N_HEADS, HEAD_DIM), cos, sin)
        k = partial_rope(k.reshape(B, S, 2 * N_HEADS, HEAD_DIM), cos, sin)
        v = v.reshape(B, S, N_HEADS, V_HEAD_DIM)
        lam = (jnp.exp(jnp.sum(lambda_q1[l].astype(jnp.float32) * lambda_k1[l].astype(jnp.float32)))
               - jnp.exp(jnp.sum(lambda_q2[l].astype(jnp.float32) * lambda_k2[l].astype(jnp.float32)))
               + lam_init)
        o = diff_attention(q, k, v, lam)
        o = rmsnorm(o, attn_subln[l]) * (1.0 - lam_init)
        y_a = o.reshape(B, S, V_WIDTH) @ w_attn_out[l]

        u_val, u_gate = jnp.split(u, 2, axis=-1)
        c = u_val * jax.nn.sigmoid(u_gate)
        c = causal_depthwise_conv(c, conv_w[l], conv_b[l])
        c = jax.nn.silu(layernorm(c, conv_ln_g[l], conv_ln_b[l]))
        y_b = c @ w_conv_out[l]

        g_a, g_b = jnp.split(jax.nn.sigmoid(g + b_gate[l]), 2, axis=-1)
        x = x + (g_a * y_a + g_b * y_b) @ w_out[l]

        x = x + 0.5 * swiglu(rmsnorm(x, ffn2_norm[l]), ffn2_w_gate_up[l], ffn2_w_down[l])
    return rmsnorm(x, final_norm)
```

```python
import functools
import math

import jax
import jax.numpy as jnp
from jax import lax
from jax.experimental import pallas as pl
from jax.experimental.pallas import tpu as pltpu

N_HEADS = 8
HEAD_DIM = 64
V_HEAD_DIM = 2 * HEAD_DIM
ROPE_THETA = 500000.0
ROT_DIM = HEAD_DIM // 4
CONV_WIDTH = 31
NORM_EPS = 1e-5
NEG_INF = -1e30
LAMBDA_INIT = 0.8 - 0.6 * math.exp(-0.3 * 0)

LANES = 128
CONV_HALO = 32
VMEM_LIMIT = 56 * 1024 * 1024

BF16 = jnp.bfloat16
F32 = jnp.float32


def _rms(x, gain):
    return x * lax.rsqrt(jnp.mean(x * x, axis=-1, keepdims=True) + NORM_EPS) * gain


def _const_spec(shape):
    return pl.BlockSpec(shape, lambda *_: (0,) * len(shape))


def _ffn_kernel(x_ref, g_ref, wgu_ref, wd_ref, fg_ref, o_ref, *, d_ff, chunk, final_norm):
    x = x_ref[...]
    xn = _rms(x, g_ref[...]).astype(BF16)
    y = jnp.zeros(x.shape, F32)
    for c in range(d_ff // chunk):
        a = jnp.dot(xn, wgu_ref[:, c * chunk:(c + 1) * chunk], preferred_element_type=F32)
        b = jnp.dot(xn, wgu_ref[:, d_ff + c * chunk:d_ff + (c + 1) * chunk], preferred_element_type=F32)
        act = (a * jax.nn.sigmoid(a) * b).astype(BF16)
        y = y + jnp.dot(act, wd_ref[c * chunk:(c + 1) * chunk, :], preferred_element_type=F32)
    out = x + 0.5 * y
    if final_norm:
        out = _rms(out, fg_ref[...])
    o_ref[...] = out


def _ffn(x, gain, wgu, wd, final_gain, *, final_norm, tm=512, chunk=256):
    n, d = x.shape
    d_ff = wd.shape[0]
    row = pl.BlockSpec((tm, d), lambda i: (i, 0))
    return pl.pallas_call(
        functools.partial(_ffn_kernel, d_ff=d_ff, chunk=chunk, final_norm=final_norm),
        grid=(n // tm,),
        in_specs=[row, _const_spec((1, d)), _const_spec(wgu.shape), _const_spec(wd.shape), _const_spec((1, d))],
        out_specs=row,
        out_shape=jax.ShapeDtypeStruct((n, d), F32),
        compiler_params=pltpu.CompilerParams(dimension_semantics=("arbitrary",), vmem_limit_bytes=VMEM_LIMIT),
        name="ffn_final" if final_norm else "ffn",
    )(x, gain, wgu, wd, final_gain)


def _inproj_kernel(x_ref, g_ref, w_ref, bg_ref, ct_ref, s1_ref, s2_ref,
                   q_ref, k_ref, v_ref, c_ref, gate_ref, *, d, chunk):
    hn = _rms(x_ref[...], g_ref[...]).astype(BF16)
    ct, s1, s2 = ct_ref[...], s1_ref[...], s2_ref[...]

    def proj(lo):
        return jnp.dot(hn, w_ref[:, lo:lo + chunk], preferred_element_type=F32)

    def rope(z):
        blocks = []
        for b in range(chunk // LANES):
            zb = z[:, b * LANES:(b + 1) * LANES]
            up = pltpu.roll(zb, LANES - ROT_DIM // 2, 1)
            dn = pltpu.roll(zb, ROT_DIM // 2, 1)
            blocks.append(zb * ct + up * s1 + dn * s2)
        return jnp.concatenate(blocks, axis=-1)

    for c in range(d // chunk):
        lo = c * chunk
        q_ref[:, lo:lo + chunk] = (rope(proj(lo)) * (HEAD_DIM ** -0.5)).astype(BF16)
        k_ref[:, lo:lo + chunk] = rope(proj(d + lo)).astype(BF16)
        v_ref[:, lo:lo + chunk] = proj(2 * d + lo).astype(BF16)
        c_ref[:, lo:lo + chunk] = proj(3 * d + lo) * jax.nn.sigmoid(proj(4 * d + lo))
    for c in range(2 * d // chunk):
        lo = c * chunk
        gate_ref[:, lo:lo + chunk] = jax.nn.sigmoid(proj(5 * d + lo) + bg_ref[:, lo:lo + chunk])


def _inproj(x, gain, w_in, b_gate, ct, s1, s2, *, seq, tm=512, chunk=512):
    n, d = x.shape
    row = pl.BlockSpec((tm, d), lambda i: (i, 0))
    tab = pl.BlockSpec((tm, LANES), lambda i: (i % (seq // tm), 0))
    return pl.pallas_call(
        functools.partial(_inproj_kernel, d=d, chunk=chunk),
        grid=(n // tm,),
        in_specs=[row, _const_spec((1, d)), _const_spec(w_in.shape), _const_spec((1, 2 * d)), tab, tab, tab],
        out_specs=[row, row, row, row, pl.BlockSpec((tm, 2 * d), lambda i: (i, 0))],
        out_shape=[jax.ShapeDtypeStruct((n, d), BF16)] * 3
        + [jax.ShapeDtypeStruct((n, d), F32), jax.ShapeDtypeStruct((n, 2 * d), F32)],
        compiler_params=pltpu.CompilerParams(dimension_semantics=("arbitrary",), vmem_limit_bytes=VMEM_LIMIT),
        name="inproj",
    )(x, gain, w_in, b_gate, ct, s1, s2)


def _attn_kernel(lam_ref, g_ref, q_ref, k_ref, v_ref, o_ref, m_ref, l_ref, acc_ref, *, tq, tk):
    i = pl.program_id(2)
    q = q_ref[...]
    lane = lax.broadcasted_iota(jnp.int32, q.shape, 1)
    qs = (jnp.where(lane < HEAD_DIM, q, jnp.zeros_like(q)), jnp.where(lane >= HEAD_DIM, q, jnp.zeros_like(q)))

    m_ref[...] = jnp.full(m_ref.shape, NEG_INF, F32)
    l_ref[...] = jnp.zeros(l_ref.shape, F32)
    acc_ref[...] = jnp.zeros(acc_ref.shape, F32)

    def step(j, masked):
        start = pl.multiple_of(j * tk, tk)
        kc = k_ref[pl.ds(start, tk), :]
        vc = v_ref[pl.ds(start, tk), :]
        for c in range(2):
            s = lax.dot_general(qs[c], kc, (((1,), (1,)), ((), ())), preferred_element_type=F32)
            if masked:
                row = lax.broadcasted_iota(jnp.int32, s.shape, 0)
                col = lax.broadcasted_iota(jnp.int32, s.shape, 1)
                s = jnp.where(col <= row, s, NEG_INF)
            m_prev = m_ref[c]
            m_new = jnp.maximum(m_prev, jnp.max(s, axis=-1, keepdims=True))
            p = jnp.exp(s - m_new)
            alpha = jnp.exp(m_prev - m_new)
            l_ref[c] = alpha * l_ref[c] + jnp.sum(p, axis=-1, keepdims=True)
            acc_ref[c] = alpha * acc_ref[c] + jnp.dot(p.astype(BF16), vc, preferred_element_type=F32)
            m_ref[c] = m_new

    def body(j, carry):
        step(j, False)
        return carry

    lax.fori_loop(0, i, body, 0)
    step(i, True)

    lp = lam_ref[...]
    lam = (jnp.exp(jnp.sum(lp[0:1] * lp[1:2], axis=-1, keepdims=True))
           - jnp.exp(jnp.sum(lp[2:3] * lp[3:4], axis=-1, keepdims=True)) + LAMBDA_INIT)
    o = acc_ref[0] / l_ref[0] - lam * (acc_ref[1] / l_ref[1])
    o_ref[...] = (_rms(o, g_ref[...]) * (1.0 - LAMBDA_INIT)).astype(BF16)


def _attention(lam_params, subln, q, k, v, *, tq=512):
    b, s, d = q.shape
    tk = tq
    qo = pl.BlockSpec((None, tq, V_HEAD_DIM), lambda bi, h, i: (bi, i, h))
    kv = pl.BlockSpec((None, s, V_HEAD_DIM), lambda bi, h, i: (bi, 0, h))
    return pl.pallas_call(
        functools.partial(_attn_kernel, tq=tq, tk=tk),
        grid=(b, N_HEADS, s // tq),
        in_specs=[_const_spec(lam_params.shape), _const_spec((1, V_HEAD_DIM)), qo, kv, kv],
        out_specs=qo,
        out_shape=jax.ShapeDtypeStruct((b, s, d), BF16),
        scratch_shapes=[pltpu.VMEM((2, tq, 1), F32), pltpu.VMEM((2, tq, 1), F32),
                        pltpu.VMEM((2, tq, V_HEAD_DIM), F32)],
        compiler_params=pltpu.CompilerParams(dimension_semantics=("arbitrary",) * 3, vmem_limit_bytes=VMEM_LIMIT),
        name="diff_attn",
    )(lam_params, subln, q, k, v)


def _merge_kernel(x_ref, o_ref, c_ref, halo_ref, gate_ref, cw_ref, cb_ref, lg_ref, lb_ref,
                  wa_ref, wb_ref, wo_ref, out_ref, buf_ref, cv_ref, *, tm, d, tiles_per_seq, rows):
    first = pl.program_id(0) % tiles_per_seq == 0
    halo = halo_ref[...]
    buf_ref[0:CONV_HALO, :] = jnp.where(first, jnp.zeros_like(halo), halo)
    buf_ref[CONV_HALO:, :] = c_ref[...]

    base = CONV_HALO - (CONV_WIDTH - 1)

    for r0 in range(0, tm, rows):
        for lb in range(d // LANES):
            cols = slice(lb * LANES, (lb + 1) * LANES)
            acc = jnp.broadcast_to(cb_ref[:, cols], (rows, LANES))
            for j in range(CONV_WIDTH):
                acc = acc + cw_ref[j:j + 1, cols] * buf_ref[r0 + base + j:r0 + base + j + rows, cols]
            cv_ref[r0:r0 + rows, cols] = acc

    cv = cv_ref[...]
    mu = jnp.mean(cv, axis=-1, keepdims=True)
    cen = cv - mu
    var = jnp.mean(cen * cen, axis=-1, keepdims=True)
    y = cen * lax.rsqrt(var + NORM_EPS) * lg_ref[...] + lb_ref[...]
    y = (y * jax.nn.sigmoid(y)).astype(BF16)
    y_b = jnp.dot(y, wb_ref[...], preferred_element_type=F32)
    y_a = jnp.dot(o_ref[...], wa_ref[...], preferred_element_type=F32)
    mix = (gate_ref[:, :d] * y_a + gate_ref[:, d:] * y_b).astype(BF16)
    out_ref[...] = x_ref[...] + jnp.dot(mix, wo_ref[...], preferred_element_type=F32)


def _merge(x, o, c, gates, conv_w, conv_b, ln_g, ln_b, wa, wb, wo, *, seq, tm=512, rows=64):
    n, d = x.shape
    row = pl.BlockSpec((tm, d), lambda i: (i, 0))
    halo = pl.BlockSpec((CONV_HALO, d), lambda i: (jnp.maximum(i * (tm // CONV_HALO) - 1, 0), 0))
    vec = _const_spec((1, d))
    mat = _const_spec((d, d))
    return pl.pallas_call(
        functools.partial(_merge_kernel, tm=tm, d=d, tiles_per_seq=seq // tm, rows=rows),
        grid=(n // tm,),
        in_specs=[row, row, row, halo, pl.BlockSpec((tm, 2 * d), lambda i: (i, 0)),
                  _const_spec(conv_w.shape), vec, vec, vec, mat, mat, mat],
        out_specs=row,
        out_shape=jax.ShapeDtypeStruct((n, d), F32),
        scratch_shapes=[pltpu.VMEM((CONV_HALO + tm, d), F32), pltpu.VMEM((tm, d), F32)],
        compiler_params=pltpu.CompilerParams(dimension_semantics=("arbitrary",), vmem_limit_bytes=VMEM_LIMIT),
        name="conv_merge",
    )(x, o, c, c, gates, conv_w, conv_b, ln_g, ln_b, wa, wb, wo)


def _rope_tables(seq):
    pos = jnp.arange(seq, dtype=F32)
    inv_freq = ROPE_THETA ** (-jnp.arange(0, ROT_DIM, 2, dtype=F32) / ROT_DIM)
    ang = pos[:, None] * inv_freq[None, :]
    cos, sin = jnp.cos(ang), jnp.sin(ang)
    half = ROT_DIM // 2
    pad = HEAD_DIM - ROT_DIM
    ones = jnp.ones((seq, pad), F32)
    zeros_h = jnp.zeros((seq, half), F32)
    zeros_p = jnp.zeros((seq, pad), F32)
    ct = jnp.concatenate([cos, cos, ones], axis=-1)
    s1 = jnp.concatenate([-sin, zeros_h, zeros_p], axis=-1)
    s2 = jnp.concatenate([zeros_h, sin, zeros_p], axis=-1)
    rep = LANES // HEAD_DIM
    return tuple(jnp.tile(t, (1, rep)) for t in (ct, s1, s2))


def kernel(x, ffn1_norm, ffn1_w_gate_up, ffn1_w_down, mix_norm, w_in, b_gate, lambda_q1, lambda_k1, lambda_q2, lambda_k2, attn_subln, w_attn_out, conv_w, conv_b, conv_ln_g, conv_ln_b, w_conv_out, w_out, ffn2_norm, ffn2_w_gate_up, ffn2_w_down, final_norm):
    bsz, seq, d = x.shape
    n = bsz * seq
    l = 0
    xf = x.reshape(n, d)
    ct, s1, s2 = _rope_tables(seq)
    final_gain = final_norm.reshape(1, d)

    x1 = _ffn(xf, ffn1_norm[l].reshape(1, d), ffn1_w_gate_up[l].astype(BF16), ffn1_w_down[l].astype(BF16),
              final_gain, final_norm=False)
    q, k, v, c, gates = _inproj(x1, mix_norm[l].reshape(1, d), w_in[l].astype(BF16), b_gate[l].reshape(1, 2 * d),
                                ct, s1, s2, seq=seq)
    lam_params = jnp.stack([lambda_q1[l], lambda_k1[l], lambda_q2[l], lambda_k2[l]]).astype(F32)
    o = _attention(lam_params, attn_subln[l].reshape(1, V_HEAD_DIM),
                   q.reshape(bsz, seq, d), k.reshape(bsz, seq, d), v.reshape(bsz, seq, d))
    x2 = _merge(x1, o.reshape(n, d), c, gates, conv_w[l], conv_b[l].reshape(1, d),
                conv_ln_g[l].reshape(1, d), conv_ln_b[l].reshape(1, d),
                w_attn_out[l].astype(BF16), w_conv_out[l].astype(BF16), w_out[l].astype(BF16), seq=seq)
    x3 = _ffn(x2, ffn2_norm[l].reshape(1, d), ffn2_w_gate_up[l].astype(BF16), ffn2_w_down[l].astype(BF16),
              final_gain, final_norm=True)
    return x3.reshape(bsz, seq, d)
```

```python
import functools
import math

import jax
import jax.numpy as jnp
from jax import lax
from jax.experimental import pallas as pl
from jax.experimental.pallas import tpu as pltpu

N_HEADS = 8
HEAD_DIM = 64
V_HEAD_DIM = 2 * HEAD_DIM
ROPE_THETA = 500000.0
ROT_DIM = HEAD_DIM // 4
CONV_WIDTH = 31
NORM_EPS = 1e-5
NEG_INF = -1e30
LAMBDA_INIT = 0.8 - 0.6 * math.exp(-0.3 * 0)
Q_SCALE = HEAD_DIM ** -0.5 * math.log2(math.e)

LANES = 128
CONV_HALO = 32
VMEM_LIMIT = 56 * 1024 * 1024

BF16 = jnp.bfloat16
F32 = jnp.float32


def _rms(x, gain):
    return x * lax.rsqrt(jnp.mean(x * x, axis=-1, keepdims=True) + NORM_EPS) * gain


def _const_spec(shape):
    return pl.BlockSpec(shape, lambda *_: (0,) * len(shape))


def _ffn_kernel(x_ref, g_ref, wgu_ref, wd_ref, fg_ref, o_ref, *, d_ff, chunk, final_norm):
    x = x_ref[...]
    xn = _rms(x, g_ref[...]).astype(BF16)
    y = jnp.zeros(x.shape, F32)
    for c in range(d_ff // chunk):
        a = jnp.dot(xn, wgu_ref[:, c * chunk:(c + 1) * chunk], preferred_element_type=F32)
        b = jnp.dot(xn, wgu_ref[:, d_ff + c * chunk:d_ff + (c + 1) * chunk], preferred_element_type=F32)
        act = (a * jax.nn.sigmoid(a) * b).astype(BF16)
        y = y + jnp.dot(act, wd_ref[c * chunk:(c + 1) * chunk, :], preferred_element_type=F32)
    out = x + 0.5 * y
    if final_norm:
        out = _rms(out, fg_ref[...])
    o_ref[...] = out


def _ffn(x, gain, wgu, wd, final_gain, *, final_norm, tm=512, chunk=256):
    n, d = x.shape
    d_ff = wd.shape[0]
    row = pl.BlockSpec((tm, d), lambda i: (i, 0))
    return pl.pallas_call(
        functools.partial(_ffn_kernel, d_ff=d_ff, chunk=chunk, final_norm=final_norm),
        grid=(n // tm,),
        in_specs=[row, _const_spec((1, d)), _const_spec(wgu.shape), _const_spec(wd.shape), _const_spec((1, d))],
        out_specs=row,
        out_shape=jax.ShapeDtypeStruct((n, d), F32),
        compiler_params=pltpu.CompilerParams(dimension_semantics=("arbitrary",), vmem_limit_bytes=VMEM_LIMIT),
        name="ffn_final" if final_norm else "ffn",
    )(x, gain, wgu, wd, final_gain)


def _inproj_kernel(x_ref, g_ref, w_ref, bg_ref, ct_ref, s1_ref, s2_ref,
                   q_ref, k_ref, vt_ref, c_ref, gate_ref, *, d, chunk):
    hn = _rms(x_ref[...], g_ref[...]).astype(BF16)
    ct, s1, s2 = ct_ref[...], s1_ref[...], s2_ref[...]

    def proj(lo):
        return jnp.dot(hn, w_ref[:, lo:lo + chunk], preferred_element_type=F32)

    def rope(z):
        blocks = []
        for b in range(chunk // LANES):
            zb = z[:, b * LANES:(b + 1) * LANES]
            up = pltpu.roll(zb, LANES - ROT_DIM // 2, 1)
            dn = pltpu.roll(zb, ROT_DIM // 2, 1)
            blocks.append(zb * ct + up * s1 + dn * s2)
        return jnp.concatenate(blocks, axis=-1)

    for c in range(d // chunk):
        lo = c * chunk
        q_ref[:, lo:lo + chunk] = (rope(proj(lo)) * Q_SCALE).astype(BF16)
        k_ref[:, lo:lo + chunk] = rope(proj(d + lo)).astype(BF16)
        vz = proj(2 * d + lo)
        for b in range(chunk // V_HEAD_DIM):
            h = (lo + b * V_HEAD_DIM) // V_HEAD_DIM
            vt_ref[0, h] = vz[:, b * V_HEAD_DIM:(b + 1) * V_HEAD_DIM].T.astype(BF16)
        c_ref[:, lo:lo + chunk] = proj(3 * d + lo) * jax.nn.sigmoid(proj(4 * d + lo))
    for c in range(2 * d // chunk):
        lo = c * chunk
        gate_ref[:, lo:lo + chunk] = jax.nn.sigmoid(proj(5 * d + lo) + bg_ref[:, lo:lo + chunk])


def _inproj(x, gain, w_in, b_gate, ct, s1, s2, *, seq, tm=512, chunk=512):
    n, d = x.shape
    row = pl.BlockSpec((tm, d), lambda i: (i, 0))
    tab = pl.BlockSpec((tm, LANES), lambda i: (i % (seq // tm), 0))
    return pl.pallas_call(
        functools.partial(_inproj_kernel, d=d, chunk=chunk),
        grid=(n // tm,),
        in_specs=[row, _const_spec((1, d)), _const_spec(w_in.shape), _const_spec((1, 2 * d)), tab, tab, tab],
        out_specs=[row, row, pl.BlockSpec((1, N_HEADS, V_HEAD_DIM, tm), lambda i: (i, 0, 0, 0)),
                   row, pl.BlockSpec((tm, 2 * d), lambda i: (i, 0))],
        out_shape=[jax.ShapeDtypeStruct((n, d), BF16)] * 2
        + [jax.ShapeDtypeStruct((n // tm, N_HEADS, V_HEAD_DIM, tm), BF16),
           jax.ShapeDtypeStruct((n, d), F32), jax.ShapeDtypeStruct((n, 2 * d), F32)],
        compiler_params=pltpu.CompilerParams(dimension_semantics=("arbitrary",), vmem_limit_bytes=VMEM_LIMIT),
        name="inproj",
    )(x, gain, w_in, b_gate, ct, s1, s2)


def _attn_kernel(lam_ref, g_ref, q_ref, k_ref, vt_ref, o_ref, m_ref, l_ref, acc_ref, sa_ref, sb_ref, *, tq, tk):
    i = pl.program_id(2)
    q = q_ref[...]
    lane = lax.broadcasted_iota(jnp.int32, q.shape, 1)
    qs = (jnp.where(lane < HEAD_DIM, q, jnp.zeros_like(q)), jnp.where(lane >= HEAD_DIM, q, jnp.zeros_like(q)))

    m_ref[...] = jnp.full(m_ref.shape, NEG_INF, F32)
    l_ref[...] = jnp.zeros(l_ref.shape, F32)
    acc_ref[...] = jnp.zeros(acc_ref.shape, F32)

    def scores(j, s_ref):
        kc = k_ref[pl.ds(pl.multiple_of(j * tk, tk), tk), :]
        for c in range(2):
            s_ref[c] = lax.dot_general(kc, qs[c], (((1,), (1,)), ((), ())), preferred_element_type=F32)

    def consume(j, s_ref, masked):
        vt = vt_ref[j]
        for c in range(2):
            s = s_ref[c]
            if masked:
                key = lax.broadcasted_iota(jnp.int32, s.shape, 0)
                qry = lax.broadcasted_iota(jnp.int32, s.shape, 1)
                s = jnp.where(key <= qry, s, NEG_INF)
            m_prev = m_ref[c]
            m_new = jnp.maximum(m_prev, jnp.max(s, axis=0, keepdims=True))
            p = jnp.exp2(s - m_new)
            alpha = jnp.exp2(m_prev - m_new)
            l_ref[c] = alpha * l_ref[c] + jnp.sum(p, axis=0, keepdims=True)
            acc_ref[c] = alpha * acc_ref[c] + jnp.dot(vt, p.astype(BF16), preferred_element_type=F32)
            m_ref[c] = m_new

    scores(0, sa_ref)

    def pair(t, carry):
        scores(2 * t + 1, sb_ref)
        consume(2 * t, sa_ref, False)
        scores(2 * t + 2, sa_ref)
        consume(2 * t + 1, sb_ref, False)
        return carry

    lax.fori_loop(0, i // 2, pair, 0)

    @pl.when(i % 2 == 0)
    def _():
        consume(i, sa_ref, True)

    @pl.when(i % 2 == 1)
    def _():
        scores(i, sb_ref)
        consume(i - 1, sa_ref, False)
        consume(i, sb_ref, True)

    lp = lam_ref[...]
    lam = (jnp.exp(jnp.sum(lp[0:1] * lp[1:2], axis=-1, keepdims=True))
           - jnp.exp(jnp.sum(lp[2:3] * lp[3:4], axis=-1, keepdims=True)) + LAMBDA_INIT)
    ot = acc_ref[0] / l_ref[0] - lam * (acc_ref[1] / l_ref[1])
    ot = ot * lax.rsqrt(jnp.mean(ot * ot, axis=0, keepdims=True) + NORM_EPS)
    o_ref[...] = (ot.T * g_ref[...] * (1.0 - LAMBDA_INIT)).astype(BF16)


def _attention(lam_params, subln, q, k, vt, *, tq=512):
    b, s, d = q.shape
    tk = vt.shape[-1]
    assert tq == tk
    qo = pl.BlockSpec((None, tq, V_HEAD_DIM), lambda bi, h, i: (bi, i, h))
    kspec = pl.BlockSpec((None, s, V_HEAD_DIM), lambda bi, h, i: (bi, 0, h))
    vspec = pl.BlockSpec((s // tk, None, V_HEAD_DIM, tk), lambda bi, h, i: (bi, h, 0, 0))
    return pl.pallas_call(
        functools.partial(_attn_kernel, tq=tq, tk=tk),
        grid=(b, N_HEADS, s // tq),
        in_specs=[_const_spec(lam_params.shape), _const_spec((1, V_HEAD_DIM)), qo, kspec, vspec],
        out_specs=qo,
        out_shape=jax.ShapeDtypeStruct((b, s, d), BF16),
        scratch_shapes=[pltpu.VMEM((2, 1, tq), F32), pltpu.VMEM((2, 1, tq), F32),
                        pltpu.VMEM((2, V_HEAD_DIM, tq), F32), pltpu.VMEM((2, tk, tq), F32), pltpu.VMEM((2, tk, tq), F32)],
        compiler_params=pltpu.CompilerParams(dimension_semantics=("arbitrary",) * 3, vmem_limit_bytes=VMEM_LIMIT),
        name="diff_attn",
    )(lam_params, subln, q, k, vt)


def _merge_kernel(x_ref, o_ref, c_ref, halo_ref, gate_ref, cw_ref, cb_ref, lg_ref, lb_ref,
                  wa_ref, wb_ref, wo_ref, out_ref, buf_ref, cv_ref, *, tm, d, tiles_per_seq, rows):
    first = pl.program_id(0) % tiles_per_seq == 0
    halo = halo_ref[...]
    buf_ref[0:CONV_HALO, :] = jnp.where(first, jnp.zeros_like(halo), halo)
    buf_ref[CONV_HALO:, :] = c_ref[...]

    base = CONV_HALO - (CONV_WIDTH - 1)

    for r0 in range(0, tm, rows):
        for lb in range(d // LANES):
            cols = slice(lb * LANES, (lb + 1) * LANES)
            acc = jnp.broadcast_to(cb_ref[:, cols], (rows, LANES))
            for j in range(CONV_WIDTH):
                acc = acc + cw_ref[j:j + 1, cols] * buf_ref[r0 + base + j:r0 + base + j + rows, cols]
            cv_ref[r0:r0 + rows, cols] = acc

    cv = cv_ref[...]
    mu = jnp.mean(cv, axis=-1, keepdims=True)
    cen = cv - mu
    var = jnp.mean(cen * cen, axis=-1, keepdims=True)
    y = cen * lax.rsqrt(var + NORM_EPS) * lg_ref[...] + lb_ref[...]
    y = (y * jax.nn.sigmoid(y)).astype(BF16)
    y_b = jnp.dot(y, wb_ref[...], preferred_element_type=F32)
    y_a = jnp.dot(o_ref[...], wa_ref[...], preferred_element_type=F32)
    mix = (gate_ref[:, :d] * y_a + gate_ref[:, d:] * y_b).astype(BF16)
    out_ref[...] = x_ref[...] + jnp.dot(mix, wo_ref[...], preferred_element_type=F32)


def _merge(x, o, c, gates, conv_w, conv_b, ln_g, ln_b, wa, wb, wo, *, seq, tm=512, rows=64):
    n, d = x.shape
    row = pl.BlockSpec((tm, d), lambda i: (i, 0))
    halo = pl.BlockSpec((CONV_HALO, d), lambda i: (jnp.maximum(i * (tm // CONV_HALO) - 1, 0), 0))
    vec = _const_spec((1, d))
    mat = _const_spec((d, d))
    return pl.pallas_call(
        functools.partial(_merge_kernel, tm=tm, d=d, tiles_per_seq=seq // tm, rows=rows),
        grid=(n // tm,),
        in_specs=[row, row, row, halo, pl.BlockSpec((tm, 2 * d), lambda i: (i, 0)),
                  _const_spec(conv_w.shape), vec, vec, vec, mat, mat, mat],
        out_specs=row,
        out_shape=jax.ShapeDtypeStruct((n, d), F32),
        scratch_shapes=[pltpu.VMEM((CONV_HALO + tm, d), F32), pltpu.VMEM((tm, d), F32)],
        compiler_params=pltpu.CompilerParams(dimension_semantics=("arbitrary",), vmem_limit_bytes=VMEM_LIMIT),
        name="conv_merge",
    )(x, o, c, c, gates, conv_w, conv_b, ln_g, ln_b, wa, wb, wo)


def _rope_tables(seq):
    pos = jnp.arange(seq, dtype=F32)
    inv_freq = ROPE_THETA ** (-jnp.arange(0, ROT_DIM, 2, dtype=F32) / ROT_DIM)
    ang = pos[:, None] * inv_freq[None, :]
    cos, sin = jnp.cos(ang), jnp.sin(ang)
    half = ROT_DIM // 2
    pad = HEAD_DIM - ROT_DIM
    ones = jnp.ones((seq, pad), F32)
    zeros_h = jnp.zeros((seq, half), F32)
    zeros_p = jnp.zeros((seq, pad), F32)
    ct = jnp.concatenate([cos, cos, ones], axis=-1)
    s1 = jnp.concatenate([-sin, zeros_h, zeros_p], axis=-1)
    s2 = jnp.concatenate([zeros_h, sin, zeros_p], axis=-1)
    rep = LANES // HEAD_DIM
    return tuple(jnp.tile(t, (1, rep)) for t in (ct, s1, s2))


def kernel(x, ffn1_norm, ffn1_w_gate_up, ffn1_w_down, mix_norm, w_in, b_gate, lambda_q1, lambda_k1, lambda_q2, lambda_k2, attn_subln, w_attn_out, conv_w, conv_b, conv_ln_g, conv_ln_b, w_conv_out, w_out, ffn2_norm, ffn2_w_gate_up, ffn2_w_down, final_norm):
    bsz, seq, d = x.shape
    n = bsz * seq
    l = 0
    xf = x.reshape(n, d)
    ct, s1, s2 = _rope_tables(seq)
    final_gain = final_norm.reshape(1, d)

    x1 = _ffn(xf, ffn1_norm[l].reshape(1, d), ffn1_w_gate_up[l].astype(BF16), ffn1_w_down[l].astype(BF16),
              final_gain, final_norm=False)
    q, k, vt, c, gates = _inproj(x1, mix_norm[l].reshape(1, d), w_in[l].astype(BF16), b_gate[l].reshape(1, 2 * d),
                                ct, s1, s2, seq=seq)
    lam_params = jnp.stack([lambda_q1[l], lambda_k1[l], lambda_q2[l], lambda_k2[l]]).astype(F32)
    o = _attention(lam_params, attn_subln[l].reshape(1, V_HEAD_DIM),
                   q.reshape(bsz, seq, d), k.reshape(bsz, seq, d), vt)
    x2 = _merge(x1, o.reshape(n, d), c, gates, conv_w[l], conv_b[l].reshape(1, d),
                conv_ln_g[l].reshape(1, d), conv_ln_b[l].reshape(1, d),
                w_attn_out[l].astype(BF16), w_conv_out[l].astype(BF16), w_out[l].astype(BF16), seq=seq)
    x3 = _ffn(x2, ffn2_norm[l].reshape(1, d), ffn2_w_gate_up[l].astype(BF16), ffn2_w_down[l].astype(BF16),
              final_gain, final_norm=True)
    return x3.reshape(bsz, seq, d)
```

```python
import functools
import math

import jax
import jax.numpy as jnp
from jax import lax
from jax.experimental import pallas as pl
from jax.experimental.pallas import tpu as pltpu

N_HEADS = 8
HEAD_DIM = 64
V_HEAD_DIM = 2 * HEAD_DIM
ROPE_THETA = 500000.0
ROT_DIM = HEAD_DIM // 4
CONV_WIDTH = 31
NORM_EPS = 1e-5
NEG_INF = -1e30
LAMBDA_INIT = 0.8 - 0.6 * math.exp(-0.3 * 0)
Q_SCALE = HEAD_DIM ** -0.5 * math.log2(math.e)

LANES = 128
SUBLANES = 8
BF16_ROWS = 16
VT_ROWS = 2 * 64 + BF16_ROWS
CONV_HALO = 32
VMEM_LIMIT = 56 * 1024 * 1024

BF16 = jnp.bfloat16
F32 = jnp.float32


def _rms(x, gain):
    return x * lax.rsqrt(jnp.mean(x * x, axis=-1, keepdims=True) + NORM_EPS) * gain


def _const_spec(shape):
    return pl.BlockSpec(shape, lambda *_: (0,) * len(shape))


def _ffn_kernel(x_ref, g_ref, wgu_ref, wd_ref, fg_ref, o_ref, *, d_ff, chunk, final_norm):
    x = x_ref[...]
    xn = _rms(x, g_ref[...]).astype(BF16)
    y = jnp.zeros(x.shape, F32)
    for c in range(d_ff // chunk):
        a = jnp.dot(xn, wgu_ref[:, c * chunk:(c + 1) * chunk], preferred_element_type=F32)
        b = jnp.dot(xn, wgu_ref[:, d_ff + c * chunk:d_ff + (c + 1) * chunk], preferred_element_type=F32)
        act = (a * jax.nn.sigmoid(a) * b).astype(BF16)
        y = y + jnp.dot(act, wd_ref[c * chunk:(c + 1) * chunk, :], preferred_element_type=F32)
    out = x + 0.5 * y
    if final_norm:
        out = _rms(out, fg_ref[...])
    o_ref[...] = out


def _ffn(x, gain, wgu, wd, final_gain, *, final_norm, tm=512, chunk=256):
    n, d = x.shape
    d_ff = wd.shape[0]
    row = pl.BlockSpec((tm, d), lambda i: (i, 0))
    return pl.pallas_call(
        functools.partial(_ffn_kernel, d_ff=d_ff, chunk=chunk, final_norm=final_norm),
        grid=(n // tm,),
        in_specs=[row, _const_spec((1, d)), _const_spec(wgu.shape), _const_spec(wd.shape), _const_spec((1, d))],
        out_specs=row,
        out_shape=jax.ShapeDtypeStruct((n, d), F32),
        compiler_params=pltpu.CompilerParams(dimension_semantics=("arbitrary",), vmem_limit_bytes=VMEM_LIMIT),
        name="ffn_final" if final_norm else "ffn",
    )(x, gain, wgu, wd, final_gain)


def _inproj_kernel(x_ref, g_ref, w_ref, bg_ref, ct_ref, s1_ref, s2_ref,
                   q_ref, k_ref, vt_ref, c_ref, gate_ref, *, d, chunk):
    hn = _rms(x_ref[...], g_ref[...]).astype(BF16)
    ct, s1, s2 = ct_ref[...], s1_ref[...], s2_ref[...]
    tm = hn.shape[0]
    ones_rows = (lax.broadcasted_iota(jnp.int32, (BF16_ROWS, tm), 0) == 0).astype(BF16)

    def proj(lo):
        return jnp.dot(hn, w_ref[:, lo:lo + chunk], preferred_element_type=F32)

    def rope(z):
        blocks = []
        for b in range(chunk // LANES):
            zb = z[:, b * LANES:(b + 1) * LANES]
            up = pltpu.roll(zb, LANES - ROT_DIM // 2, 1)
            dn = pltpu.roll(zb, ROT_DIM // 2, 1)
            blocks.append(zb * ct + up * s1 + dn * s2)
        return jnp.concatenate(blocks, axis=-1)

    for c in range(d // chunk):
        lo = c * chunk
        q_ref[:, lo:lo + chunk] = (rope(proj(lo)) * Q_SCALE).astype(BF16)
        k_ref[:, lo:lo + chunk] = rope(proj(d + lo)).astype(BF16)
        vz = proj(2 * d + lo)
        for b in range(chunk // V_HEAD_DIM):
            h = (lo + b * V_HEAD_DIM) // V_HEAD_DIM
            vt_ref[0, h, :V_HEAD_DIM, :] = vz[:, b * V_HEAD_DIM:(b + 1) * V_HEAD_DIM].T.astype(BF16)
            vt_ref[0, h, V_HEAD_DIM:, :] = ones_rows
        c_ref[:, lo:lo + chunk] = proj(3 * d + lo) * jax.nn.sigmoid(proj(4 * d + lo))
    for c in range(2 * d // chunk):
        lo = c * chunk
        gate_ref[:, lo:lo + chunk] = jax.nn.sigmoid(proj(5 * d + lo) + bg_ref[:, lo:lo + chunk])


def _inproj(x, gain, w_in, b_gate, ct, s1, s2, *, seq, tm=512, chunk=512):
    n, d = x.shape
    row = pl.BlockSpec((tm, d), lambda i: (i, 0))
    tab = pl.BlockSpec((tm, LANES), lambda i: (i % (seq // tm), 0))
    return pl.pallas_call(
        functools.partial(_inproj_kernel, d=d, chunk=chunk),
        grid=(n // tm,),
        in_specs=[row, _const_spec((1, d)), _const_spec(w_in.shape), _const_spec((1, 2 * d)), tab, tab, tab],
        out_specs=[row, row, pl.BlockSpec((1, N_HEADS, VT_ROWS, tm), lambda i: (i, 0, 0, 0)),
                   row, pl.BlockSpec((tm, 2 * d), lambda i: (i, 0))],
        out_shape=[jax.ShapeDtypeStruct((n, d), BF16)] * 2
        + [jax.ShapeDtypeStruct((n // tm, N_HEADS, VT_ROWS, tm), BF16),
           jax.ShapeDtypeStruct((n, d), F32), jax.ShapeDtypeStruct((n, 2 * d), F32)],
        compiler_params=pltpu.CompilerParams(dimension_semantics=("arbitrary",), vmem_limit_bytes=VMEM_LIMIT),
        name="inproj",
    )(x, gain, w_in, b_gate, ct, s1, s2)


def _attn_kernel(lam_ref, g_ref, q_ref, k_ref, vt_ref, o_ref,
                 m_ref, acc_ref, sa_ref, sb_ref, xa_ref, xb_ref, *, tq, tk):
    i = pl.program_id(2)
    q = q_ref[...]
    lane = lax.broadcasted_iota(jnp.int32, q.shape, 1)
    qs = (jnp.where(lane < HEAD_DIM, q, jnp.zeros_like(q)), jnp.where(lane >= HEAD_DIM, q, jnp.zeros_like(q)))

    m_ref[...] = jnp.full(m_ref.shape, NEG_INF, F32)
    acc_ref[...] = jnp.zeros(acc_ref.shape, F32)

    def scores(j, s_ref, x_ref):
        kc = k_ref[pl.ds(pl.multiple_of(j * tk, tk), tk), :]
        for c in range(2):
            s = lax.dot_general(kc, qs[c], (((1,), (1,)), ((), ())), preferred_element_type=F32)
            s_ref[c] = s
            x_ref[c] = jnp.max(s, axis=0, keepdims=True)

    def consume(j, s_ref, x_ref, masked):
        vt = vt_ref[j]
        for c in range(2):
            s = s_ref[c]
            if masked:
                key = lax.broadcasted_iota(jnp.int32, s.shape, 0)
                qry = lax.broadcasted_iota(jnp.int32, s.shape, 1)
                s = jnp.where(key <= qry, s, NEG_INF)
                blk_max = jnp.max(s, axis=0, keepdims=True)
            else:
                blk_max = x_ref[c]
            m_prev = m_ref[c]
            m_new = jnp.maximum(m_prev, blk_max)
            p = jnp.exp2(s - m_new).astype(BF16)
            alpha = jnp.exp2(m_prev - m_new)
            acc_ref[c] = alpha * acc_ref[c] + jnp.dot(vt, p, preferred_element_type=F32)
            m_ref[c] = m_new

    scores(0, sa_ref, xa_ref)

    def pair(t, carry):
        scores(2 * t + 1, sb_ref, xb_ref)
        consume(2 * t, sa_ref, xa_ref, False)
        scores(2 * t + 2, sa_ref, xa_ref)
        consume(2 * t + 1, sb_ref, xb_ref, False)
        return carry

    lax.fori_loop(0, i // 2, pair, 0)

    @pl.when(i % 2 == 0)
    def _():
        consume(i, sa_ref, xa_ref, True)

    @pl.when(i % 2 == 1)
    def _():
        scores(i, sb_ref, xb_ref)
        consume(i - 1, sa_ref, xa_ref, False)
        consume(i, sb_ref, xb_ref, True)

    lp = lam_ref[...]
    lam = (jnp.exp(jnp.sum(lp[0:1] * lp[1:2], axis=-1, keepdims=True))
           - jnp.exp(jnp.sum(lp[2:3] * lp[3:4], axis=-1, keepdims=True)) + LAMBDA_INIT)
    ot = (acc_ref[0, :V_HEAD_DIM] / acc_ref[0, V_HEAD_DIM:V_HEAD_DIM + 1]
          - lam * (acc_ref[1, :V_HEAD_DIM] / acc_ref[1, V_HEAD_DIM:V_HEAD_DIM + 1]))
    ot = ot * lax.rsqrt(jnp.mean(ot * ot, axis=0, keepdims=True) + NORM_EPS)
    o_ref[...] = (ot.T * g_ref[...] * (1.0 - LAMBDA_INIT)).astype(BF16)


def _attention(lam_params, subln, q, k, vt, *, tq=512):
    b, s, d = q.shape
    tk = vt.shape[-1]
    assert tq == tk
    qo = pl.BlockSpec((None, tq, V_HEAD_DIM), lambda bi, h, i: (bi, i, h))
    kspec = pl.BlockSpec((None, s, V_HEAD_DIM), lambda bi, h, i: (bi, 0, h))
    vspec = pl.BlockSpec((s // tk, None, VT_ROWS, tk), lambda bi, h, i: (bi, h, 0, 0))
    return pl.pallas_call(
        functools.partial(_attn_kernel, tq=tq, tk=tk),
        grid=(b, N_HEADS, s // tq),
        in_specs=[_const_spec(lam_params.shape), _const_spec((1, V_HEAD_DIM)), qo, kspec, vspec],
        out_specs=qo,
        out_shape=jax.ShapeDtypeStruct((b, s, d), BF16),
        scratch_shapes=[pltpu.VMEM((2, 1, tq), F32), pltpu.VMEM((2, VT_ROWS, tq), F32),
                        pltpu.VMEM((2, tk, tq), F32), pltpu.VMEM((2, tk, tq), F32),
                        pltpu.VMEM((2, 1, tq), F32), pltpu.VMEM((2, 1, tq), F32)],
        compiler_params=pltpu.CompilerParams(dimension_semantics=("arbitrary",) * 3, vmem_limit_bytes=VMEM_LIMIT),
        name="diff_attn",
    )(lam_params, subln, q, k, vt)


def _merge_kernel(x_ref, o_ref, c_ref, halo_ref, gate_ref, cw_ref, cb_ref, lg_ref, lb_ref,
                  wa_ref, wb_ref, wo_ref, out_ref, buf_ref, cv_ref, *, tm, d, tiles_per_seq, rows):
    y_a = jnp.dot(o_ref[...], wa_ref[...], preferred_element_type=F32)
    first = pl.program_id(0) % tiles_per_seq == 0
    halo = halo_ref[...]
    buf_ref[0:CONV_HALO, :] = jnp.where(first, jnp.zeros_like(halo), halo)
    buf_ref[CONV_HALO:, :] = c_ref[...]

    groups = rows // SUBLANES
    for r0 in range(0, tm, rows):
        for lb in range(d // LANES):
            cols = slice(lb * LANES, (lb + 1) * LANES)
            win = buf_ref[r0:r0 + rows + CONV_HALO, cols]
            acc = jnp.broadcast_to(cb_ref[:, cols], (groups, SUBLANES, LANES))
            for r in range(SUBLANES):
                wr = win if r == 0 else pltpu.roll(win, r, 0)
                for a in range(CONV_HALO // SUBLANES):
                    shift = SUBLANES * a + r
                    if shift < CONV_WIDTH:
                        j = CONV_WIDTH - 1 - shift
                        lo = CONV_HALO - SUBLANES * a
                        w8 = jnp.broadcast_to(cw_ref[j:j + 1, cols], (SUBLANES, LANES))
                        acc = acc + w8[None] * wr[lo:lo + rows].reshape(groups, SUBLANES, LANES)
            cv_ref[r0:r0 + rows, cols] = acc.reshape(rows, LANES)

    cv = cv_ref[...]
    mu = jnp.mean(cv, axis=-1, keepdims=True)
    cen = cv - mu
    var = jnp.mean(cen * cen, axis=-1, keepdims=True)
    y = cen * lax.rsqrt(var + NORM_EPS) * lg_ref[...] + lb_ref[...]
    y = (y * jax.nn.sigmoid(y)).astype(BF16)
    y_b = jnp.dot(y, wb_ref[...], preferred_element_type=F32)
    mix = (gate_ref[:, :d] * y_a + gate_ref[:, d:] * y_b).astype(BF16)
    out_ref[...] = x_ref[...] + jnp.dot(mix, wo_ref[...], preferred_element_type=F32)


def _merge(x, o, c, gates, conv_w, conv_b, ln_g, ln_b, wa, wb, wo, *, seq, tm=512, rows=64):
    n, d = x.shape
    row = pl.BlockSpec((tm, d), lambda i: (i, 0))
    halo = pl.BlockSpec((CONV_HALO, d), lambda i: (jnp.maximum(i * (tm // CONV_HALO) - 1, 0), 0))
    vec = _const_spec((1, d))
    mat = _const_spec((d, d))
    return pl.pallas_call(
        functools.partial(_merge_kernel, tm=tm, d=d, tiles_per_seq=seq // tm, rows=rows),
        grid=(n // tm,),
        in_specs=[row, row, row, halo, pl.BlockSpec((tm, 2 * d), lambda i: (i, 0)),
                  _const_spec(conv_w.shape), vec, vec, vec, mat, mat, mat],
        out_specs=row,
        out_shape=jax.ShapeDtypeStruct((n, d), F32),
        scratch_shapes=[pltpu.VMEM((CONV_HALO + tm, d), F32), pltpu.VMEM((tm, d), F32)],
        compiler_params=pltpu.CompilerParams(dimension_semantics=("arbitrary",), vmem_limit_bytes=VMEM_LIMIT),
        name="conv_merge",
    )(x, o, c, c, gates, conv_w, conv_b, ln_g, ln_b, wa, wb, wo)


def _rope_tables(seq):
    pos = jnp.arange(seq, dtype=F32)
    inv_freq = ROPE_THETA ** (-jnp.arange(0, ROT_DIM, 2, dtype=F32) / ROT_DIM)
    ang = pos[:, None] * inv_freq[None, :]
    cos, sin = jnp.cos(ang), jnp.sin(ang)
    half = ROT_DIM // 2
    pad = HEAD_DIM - ROT_DIM
    ones = jnp.ones((seq, pad), F32)
    zeros_h = jnp.zeros((seq, half), F32)
    zeros_p = jnp.zeros((seq, pad), F32)
    ct = jnp.concatenate([cos, cos, ones], axis=-1)
    s1 = jnp.concatenate([-sin, zeros_h, zeros_p], axis=-1)
    s2 = jnp.concatenate([zeros_h, sin, zeros_p], axis=-1)
    rep = LANES // HEAD_DIM
    return tuple(jnp.tile(t, (1, rep)) for t in (ct, s1, s2))


def kernel(x, ffn1_norm, ffn1_w_gate_up, ffn1_w_down, mix_norm, w_in, b_gate, lambda_q1, lambda_k1, lambda_q2, lambda_k2, attn_subln, w_attn_out, conv_w, conv_b, conv_ln_g, conv_ln_b, w_conv_out, w_out, ffn2_norm, ffn2_w_gate_up, ffn2_w_down, final_norm):
    bsz, seq, d = x.shape
    n = bsz * seq
    l = 0
    xf = x.reshape(n, d)
    ct, s1, s2 = _rope_tables(seq)
    final_gain = final_norm.reshape(1, d)

    x1 = _ffn(xf, ffn1_norm[l].reshape(1, d), ffn1_w_gate_up[l].astype(BF16), ffn1_w_down[l].astype(BF16),
              final_gain, final_norm=False)
    q, k, vt, c, gates = _inproj(x1, mix_norm[l].reshape(1, d), w_in[l].astype(BF16), b_gate[l].reshape(1, 2 * d),
                                ct, s1, s2, seq=seq)
    lam_params = jnp.stack([lambda_q1[l], lambda_k1[l], lambda_q2[l], lambda_k2[l]]).astype(F32)
    o = _attention(lam_params, attn_subln[l].reshape(1, V_HEAD_DIM),
                   q.reshape(bsz, seq, d), k.reshape(bsz, seq, d), vt)
    x2 = _merge(x1, o.reshape(n, d), c, gates, conv_w[l], conv_b[l].reshape(1, d),
                conv_ln_g[l].reshape(1, d), conv_ln_b[l].reshape(1, d),
                w_attn_out[l].astype(BF16), w_conv_out[l].astype(BF16), w_out[l].astype(BF16), seq=seq)
    x3 = _ffn(x2, ffn2_norm[l].reshape(1, d), ffn2_w_gate_up[l].astype(BF16), ffn2_w_down[l].astype(BF16),
              final_gain, final_norm=True)
    return x3.reshape(bsz, seq, d)
```

```python
import functools
import math

import jax
import jax.numpy as jnp
from jax import lax
from jax.experimental import pallas as pl
from jax.experimental.pallas import tpu as pltpu

N_HEADS = 8
HEAD_DIM = 64
V_HEAD_DIM = 2 * HEAD_DIM
ROPE_THETA = 500000.0
ROT_DIM = HEAD_DIM // 4
CONV_WIDTH = 31
NORM_EPS = 1e-5
NEG_INF = -1e30
LAMBDA_INIT = 0.8 - 0.6 * math.exp(-0.3 * 0)
Q_SCALE = HEAD_DIM ** -0.5 * math.log2(math.e)

LANES = 128
SUBLANES = 8
BF16_ROWS = 16
VT_ROWS = 2 * 64 + BF16_ROWS
CONV_HALO = 32
VMEM_LIMIT = 56 * 1024 * 1024

BF16 = jnp.bfloat16
F32 = jnp.float32


def _rms(x, gain):
    return x * lax.rsqrt(jnp.mean(x * x, axis=-1, keepdims=True) + NORM_EPS) * gain


def _const_spec(shape):
    return pl.BlockSpec(shape, lambda *_: (0,) * len(shape))


def _ffn_kernel(x_ref, g_ref, wgu_ref, wd_ref, fg_ref, o_ref, *, d_ff, chunk, final_norm):
    x = x_ref[...]
    xn = _rms(x, g_ref[...]).astype(BF16)
    y = jnp.zeros(x.shape, F32)
    for c in range(d_ff // chunk):
        a = jnp.dot(xn, wgu_ref[:, c * chunk:(c + 1) * chunk], preferred_element_type=F32)
        b = jnp.dot(xn, wgu_ref[:, d_ff + c * chunk:d_ff + (c + 1) * chunk], preferred_element_type=F32)
        act = (a * jax.nn.sigmoid(a) * b).astype(BF16)
        y = y + jnp.dot(act, wd_ref[c * chunk:(c + 1) * chunk, :], preferred_element_type=F32)
    out = x + 0.5 * y
    if final_norm:
        out = _rms(out, fg_ref[...])
    o_ref[...] = out


def _ffn(x, gain, wgu, wd, final_gain, *, final_norm, tm=512, chunk=256):
    n, d = x.shape
    d_ff = wd.shape[0]
    row = pl.BlockSpec((tm, d), lambda i: (i, 0))
    return pl.pallas_call(
        functools.partial(_ffn_kernel, d_ff=d_ff, chunk=chunk, final_norm=final_norm),
        grid=(n // tm,),
        in_specs=[row, _const_spec((1, d)), _const_spec(wgu.shape), _const_spec(wd.shape), _const_spec((1, d))],
        out_specs=row,
        out_shape=jax.ShapeDtypeStruct((n, d), F32),
        compiler_params=pltpu.CompilerParams(dimension_semantics=("arbitrary",), vmem_limit_bytes=VMEM_LIMIT),
        name="ffn_final" if final_norm else "ffn",
    )(x, gain, wgu, wd, final_gain)


def _inproj_kernel(x_ref, g_ref, w_ref, bg_ref, ct_ref, s1_ref, s2_ref,
                   q_ref, k_ref, vt_ref, c_ref, gate_ref, *, d, chunk):
    hn = _rms(x_ref[...], g_ref[...]).astype(BF16)
    ct, s1, s2 = ct_ref[...], s1_ref[...], s2_ref[...]
    tm = hn.shape[0]
    ones_rows = (lax.broadcasted_iota(jnp.int32, (BF16_ROWS, tm), 0) == 0).astype(BF16)

    def proj(lo):
        return jnp.dot(hn, w_ref[:, lo:lo + chunk], preferred_element_type=F32)

    def rope(z):
        blocks = []
        for b in range(chunk // LANES):
            zb = z[:, b * LANES:(b + 1) * LANES]
            up = pltpu.roll(zb, LANES - ROT_DIM // 2, 1)
            dn = pltpu.roll(zb, ROT_DIM // 2, 1)
            blocks.append(zb * ct + up * s1 + dn * s2)
        return jnp.concatenate(blocks, axis=-1)

    for c in range(d // chunk):
        lo = c * chunk
        q_ref[:, lo:lo + chunk] = (rope(proj(lo)) * Q_SCALE).astype(BF16)
        k_ref[:, lo:lo + chunk] = rope(proj(d + lo)).astype(BF16)
        vz = proj(2 * d + lo)
        for b in range(chunk // V_HEAD_DIM):
            h = (lo + b * V_HEAD_DIM) // V_HEAD_DIM
            vt_ref[0, h, :V_HEAD_DIM, :] = vz[:, b * V_HEAD_DIM:(b + 1) * V_HEAD_DIM].T.astype(BF16)
            vt_ref[0, h, V_HEAD_DIM:, :] = ones_rows
        c_ref[:, lo:lo + chunk] = proj(3 * d + lo) * jax.nn.sigmoid(proj(4 * d + lo))
    for c in range(2 * d // chunk):
        lo = c * chunk
        gate_ref[:, lo:lo + chunk] = jax.nn.sigmoid(proj(5 * d + lo) + bg_ref[:, lo:lo + chunk])


def _inproj(x, gain, w_in, b_gate, ct, s1, s2, *, seq, tm=512, chunk=512):
    n, d = x.shape
    row = pl.BlockSpec((tm, d), lambda i: (i, 0))
    tab = pl.BlockSpec((tm, LANES), lambda i: (i % (seq // tm), 0))
    return pl.pallas_call(
        functools.partial(_inproj_kernel, d=d, chunk=chunk),
        grid=(n // tm,),
        in_specs=[row, _const_spec((1, d)), _const_spec(w_in.shape), _const_spec((1, 2 * d)), tab, tab, tab],
        out_specs=[row, row, pl.BlockSpec((1, N_HEADS, VT_ROWS, tm), lambda i: (i, 0, 0, 0)),
                   row, pl.BlockSpec((tm, 2 * d), lambda i: (i, 0))],
        out_shape=[jax.ShapeDtypeStruct((n, d), BF16)] * 2
        + [jax.ShapeDtypeStruct((n // tm, N_HEADS, VT_ROWS, tm), BF16),
           jax.ShapeDtypeStruct((n, d), F32), jax.ShapeDtypeStruct((n, 2 * d), F32)],
        compiler_params=pltpu.CompilerParams(dimension_semantics=("arbitrary",), vmem_limit_bytes=VMEM_LIMIT),
        name="inproj",
    )(x, gain, w_in, b_gate, ct, s1, s2)


def _attn_kernel(lam_ref, g_ref, q_ref, k_ref, vt_ref, o_ref,
                 m_ref, acc_ref, sa_ref, sb_ref, xa_ref, xb_ref, *, tq, tk, hp):
    i = pl.program_id(2)
    lane = lax.broadcasted_iota(jnp.int32, (tq, V_HEAD_DIM), 1)
    qs = []
    for hh in range(hp):
        q = q_ref[:, hh * V_HEAD_DIM:(hh + 1) * V_HEAD_DIM]
        qs.append((jnp.where(lane < HEAD_DIM, q, jnp.zeros_like(q)), jnp.where(lane >= HEAD_DIM, q, jnp.zeros_like(q))))

    m_ref[...] = jnp.full(m_ref.shape, NEG_INF, F32)
    acc_ref[...] = jnp.zeros(acc_ref.shape, F32)

    def scores(j, s_ref, x_ref, hh, c):
        kc = k_ref[pl.ds(pl.multiple_of(j * tk, tk), tk), hh * V_HEAD_DIM:(hh + 1) * V_HEAD_DIM]
        s = lax.dot_general(kc, qs[hh][c], (((1,), (1,)), ((), ())), preferred_element_type=F32)
        s_ref[2 * hh + c] = s
        x_ref[2 * hh + c] = jnp.max(s, axis=0, keepdims=True)

    def consume(j, s_ref, x_ref, hh, c, masked):
        st = 2 * hh + c
        vt = vt_ref[j, hh]
        s = s_ref[st]
        if masked:
            key = lax.broadcasted_iota(jnp.int32, s.shape, 0)
            qry = lax.broadcasted_iota(jnp.int32, s.shape, 1)
            s = jnp.where(key <= qry, s, NEG_INF)
            blk_max = jnp.max(s, axis=0, keepdims=True)
        else:
            blk_max = x_ref[st]
        m_prev = m_ref[st]
        m_new = jnp.maximum(m_prev, blk_max)
        p = jnp.exp2(s - m_new).astype(BF16)
        alpha = jnp.exp2(m_prev - m_new)
        acc_ref[st] = alpha * acc_ref[st] + jnp.dot(vt, p, preferred_element_type=F32)
        m_ref[st] = m_new

    def phase(j_next, nxt, j_cur, cur, masked=False):
        for hh in range(hp):
            for c in range(2):
                if nxt is not None:
                    scores(j_next, nxt[0], nxt[1], hh, c)
                consume(j_cur, cur[0], cur[1], hh, c, masked)

    a_bufs, b_bufs = (sa_ref, xa_ref), (sb_ref, xb_ref)
    for hh in range(hp):
        for c in range(2):
            scores(0, sa_ref, xa_ref, hh, c)

    def pair(t, carry):
        phase(2 * t + 1, b_bufs, 2 * t, a_bufs)
        phase(2 * t + 2, a_bufs, 2 * t + 1, b_bufs)
        return carry

    def quad(u, carry):
        pair(2 * u, carry)
        pair(2 * u + 1, carry)
        return carry

    lax.fori_loop(0, i // 4, quad, 0)
    lax.fori_loop(2 * (i // 4), i // 2, pair, 0)

    @pl.when(i % 2 == 0)
    def _():
        phase(None, None, i, a_bufs, masked=True)

    @pl.when(i % 2 == 1)
    def _():
        phase(i, b_bufs, i - 1, a_bufs)
        phase(None, None, i, b_bufs, masked=True)

    lp = lam_ref[...]
    lam = (jnp.exp(jnp.sum(lp[0:1] * lp[1:2], axis=-1, keepdims=True))
           - jnp.exp(jnp.sum(lp[2:3] * lp[3:4], axis=-1, keepdims=True)) + LAMBDA_INIT)
    for hh in range(hp):
        a1, a2 = acc_ref[2 * hh], acc_ref[2 * hh + 1]
        ot = (a1[:V_HEAD_DIM] / a1[V_HEAD_DIM:V_HEAD_DIM + 1]
              - lam * (a2[:V_HEAD_DIM] / a2[V_HEAD_DIM:V_HEAD_DIM + 1]))
        ot = ot * lax.rsqrt(jnp.mean(ot * ot, axis=0, keepdims=True) + NORM_EPS)
        o_ref[:, hh * V_HEAD_DIM:(hh + 1) * V_HEAD_DIM] = (ot.T * g_ref[...] * (1.0 - LAMBDA_INIT)).astype(BF16)


def _attention(lam_params, subln, q, k, vt, *, tq=512, hp=2):
    b, s, d = q.shape
    tk = vt.shape[-1]
    assert tq == tk
    w = hp * V_HEAD_DIM
    qo = pl.BlockSpec((None, tq, w), lambda bi, g, i: (bi, i, g))
    kspec = pl.BlockSpec((None, s, w), lambda bi, g, i: (bi, 0, g))
    vspec = pl.BlockSpec((s // tk, hp, VT_ROWS, tk), lambda bi, g, i: (bi, g, 0, 0))
    streams = 2 * hp
    return pl.pallas_call(
        functools.partial(_attn_kernel, tq=tq, tk=tk, hp=hp),
        grid=(b, N_HEADS // hp, s // tq),
        in_specs=[_const_spec(lam_params.shape), _const_spec((1, V_HEAD_DIM)), qo, kspec, vspec],
        out_specs=qo,
        out_shape=jax.ShapeDtypeStruct((b, s, d), BF16),
        scratch_shapes=[pltpu.VMEM((streams, 1, tq), F32), pltpu.VMEM((streams, VT_ROWS, tq), F32),
                        pltpu.VMEM((streams, tk, tq), F32), pltpu.VMEM((streams, tk, tq), F32),
                        pltpu.VMEM((streams, 1, tq), F32), pltpu.VMEM((streams, 1, tq), F32)],
        compiler_params=pltpu.CompilerParams(dimension_semantics=("arbitrary",) * 3, vmem_limit_bytes=VMEM_LIMIT),
        name="diff_attn",
    )(lam_params, subln, q, k, vt)


def _merge_kernel(x_ref, o_ref, c_ref, halo_ref, gate_ref, cw_ref, cb_ref, lg_ref, lb_ref,
                  wa_ref, wb_ref, wo_ref, out_ref, buf_ref, cv_ref, *, tm, d, tiles_per_seq, rows):
    y_a = jnp.dot(o_ref[...], wa_ref[...], preferred_element_type=F32)
    first = pl.program_id(0) % tiles_per_seq == 0
    halo = halo_ref[...]
    buf_ref[0:CONV_HALO, :] = jnp.where(first, jnp.zeros_like(halo), halo)
    buf_ref[CONV_HALO:, :] = c_ref[...]

    groups = rows // SUBLANES
    for r0 in range(0, tm, rows):
        for lb in range(d // LANES):
            cols = slice(lb * LANES, (lb + 1) * LANES)
            win = buf_ref[r0:r0 + rows + CONV_HALO, cols]
            acc = jnp.broadcast_to(cb_ref[:, cols], (groups, SUBLANES, LANES))
            for r in range(SUBLANES):
                wr = win if r == 0 else pltpu.roll(win, r, 0)
                for a in range(CONV_HALO // SUBLANES):
                    shift = SUBLANES * a + r
                    if shift < CONV_WIDTH:
                        j = CONV_WIDTH - 1 - shift
                        lo = CONV_HALO - SUBLANES * a
                        w8 = jnp.broadcast_to(cw_ref[j:j + 1, cols], (SUBLANES, LANES))
                        acc = acc + w8[None] * wr[lo:lo + rows].reshape(groups, SUBLANES, LANES)
            cv_ref[r0:r0 + rows, cols] = acc.reshape(rows, LANES)

    cv = cv_ref[...]
    mu = jnp.mean(cv, axis=-1, keepdims=True)
    cen = cv - mu
    var = jnp.mean(cen * cen, axis=-1, keepdims=True)
    y = cen * lax.rsqrt(var + NORM_EPS) * lg_ref[...] + lb_ref[...]
    y = (y * jax.nn.sigmoid(y)).astype(BF16)
    y_b = jnp.dot(y, wb_ref[...], preferred_element_type=F32)
    mix = (gate_ref[:, :d] * y_a + gate_ref[:, d:] * y_b).astype(BF16)
    out_ref[...] = x_ref[...] + jnp.dot(mix, wo_ref[...], preferred_element_type=F32)


def _merge(x, o, c, gates, conv_w, conv_b, ln_g, ln_b, wa, wb, wo, *, seq, tm=512, rows=64):
    n, d = x.shape
    row = pl.BlockSpec((tm, d), lambda i: (i, 0))
    halo = pl.BlockSpec((CONV_HALO, d), lambda i: (jnp.maximum(i * (tm // CONV_HALO) - 1, 0), 0))
    vec = _const_spec((1, d))
    mat = _const_spec((d, d))
    return pl.pallas_call(
        functools.partial(_merge_kernel, tm=tm, d=d, tiles_per_seq=seq // tm, rows=rows),
        grid=(n // tm,),
        in_specs=[row, row, row, halo, pl.BlockSpec((tm, 2 * d), lambda i: (i, 0)),
                  _const_spec(conv_w.shape), vec, vec, vec, mat, mat, mat],
        out_specs=row,
        out_shape=jax.ShapeDtypeStruct((n, d), F32),
        scratch_shapes=[pltpu.VMEM((CONV_HALO + tm, d), F32), pltpu.VMEM((tm, d), F32)],
        compiler_params=pltpu.CompilerParams(dimension_semantics=("arbitrary",), vmem_limit_bytes=VMEM_LIMIT),
        name="conv_merge",
    )(x, o, c, c, gates, conv_w, conv_b, ln_g, ln_b, wa, wb, wo)


def _rope_tables(seq):
    pos = jnp.arange(seq, dtype=F32)
    inv_freq = ROPE_THETA ** (-jnp.arange(0, ROT_DIM, 2, dtype=F32) / ROT_DIM)
    ang = pos[:, None] * inv_freq[None, :]
    cos, sin = jnp.cos(ang), jnp.sin(ang)
    half = ROT_DIM // 2
    pad = HEAD_DIM - ROT_DIM
    ones = jnp.ones((seq, pad), F32)
    zeros_h = jnp.zeros((seq, half), F32)
    zeros_p = jnp.zeros((seq, pad), F32)
    ct = jnp.concatenate([cos, cos, ones], axis=-1)
    s1 = jnp.concatenate([-sin, zeros_h, zeros_p], axis=-1)
    s2 = jnp.concatenate([zeros_h, sin, zeros_p], axis=-1)
    rep = LANES // HEAD_DIM
    return tuple(jnp.tile(t, (1, rep)) for t in (ct, s1, s2))


def kernel(x, ffn1_norm, ffn1_w_gate_up, ffn1_w_down, mix_norm, w_in, b_gate, lambda_q1, lambda_k1, lambda_q2, lambda_k2, attn_subln, w_attn_out, conv_w, conv_b, conv_ln_g, conv_ln_b, w_conv_out, w_out, ffn2_norm, ffn2_w_gate_up, ffn2_w_down, final_norm):
    bsz, seq, d = x.shape
    n = bsz * seq
    l = 0
    xf = x.reshape(n, d)
    ct, s1, s2 = _rope_tables(seq)
    final_gain = final_norm.reshape(1, d)

    x1 = _ffn(xf, ffn1_norm[l].reshape(1, d), ffn1_w_gate_up[l].astype(BF16), ffn1_w_down[l].astype(BF16),
              final_gain, final_norm=False)
    q, k, vt, c, gates = _inproj(x1, mix_norm[l].reshape(1, d), w_in[l].astype(BF16), b_gate[l].reshape(1, 2 * d),
                                ct, s1, s2, seq=seq)
    lam_params = jnp.stack([lambda_q1[l], lambda_k1[l], lambda_q2[l], lambda_k2[l]]).astype(F32)
    o = _attention(lam_params, attn_subln[l].reshape(1, V_HEAD_DIM),
                   q.reshape(bsz, seq, d), k.reshape(bsz, seq, d), vt)
    x2 = _merge(x1, o.reshape(n, d), c, gates, conv_w[l], conv_b[l].reshape(1, d),
                conv_ln_g[l].reshape(1, d), conv_ln_b[l].reshape(1, d),
                w_attn_out[l].astype(BF16), w_conv_out[l].astype(BF16), w_out[l].astype(BF16), seq=seq)
    x3 = _ffn(x2, ffn2_norm[l].reshape(1, d), ffn2_w_gate_up[l].astype(BF16), ffn2_w_down[l].astype(BF16),
              final_gain, final_norm=True)
    return x3.reshape(bsz, seq, d)
```

```python
import functools
import math

import jax
import jax.numpy as jnp
from jax import lax
from jax.experimental import pallas as pl
from jax.experimental.pallas import tpu as pltpu

N_HEADS = 8
HEAD_DIM = 64
V_HEAD_DIM = 2 * HEAD_DIM
ROPE_THETA = 500000.0
ROT_DIM = HEAD_DIM // 4
CONV_WIDTH = 31
NORM_EPS = 1e-5
NEG_INF = -1e30
LAMBDA_INIT = 0.8 - 0.6 * math.exp(-0.3 * 0)
Q_SCALE = HEAD_DIM ** -0.5 * math.log2(math.e)

LANES = 128
SUBLANES = 8
BF16_ROWS = 16
VT_ROWS = 2 * 64 + BF16_ROWS
CONV_HALO = 32
VMEM_LIMIT = 56 * 1024 * 1024

BF16 = jnp.bfloat16
F32 = jnp.float32


def _rms(x, gain):
    return x * lax.rsqrt(jnp.mean(x * x, axis=-1, keepdims=True) + NORM_EPS) * gain


def _const_spec(shape):
    return pl.BlockSpec(shape, lambda *_: (0,) * len(shape))


def _ffn_kernel(x_ref, g_ref, wgu_ref, wd_ref, fg_ref, o_ref, *, d_ff, chunk, final_norm):
    x = x_ref[...]
    xn = _rms(x, g_ref[...]).astype(BF16)
    y = jnp.zeros(x.shape, F32)
    for c in range(d_ff // chunk):
        a = jnp.dot(xn, wgu_ref[:, c * chunk:(c + 1) * chunk], preferred_element_type=F32)
        b = jnp.dot(xn, wgu_ref[:, d_ff + c * chunk:d_ff + (c + 1) * chunk], preferred_element_type=F32)
        act = (a * jax.nn.sigmoid(a) * b).astype(BF16)
        y = y + jnp.dot(act, wd_ref[c * chunk:(c + 1) * chunk, :], preferred_element_type=F32)
    out = x + 0.5 * y
    if final_norm:
        out = _rms(out, fg_ref[...])
    o_ref[...] = out


def _ffn(x, gain, wgu, wd, final_gain, *, final_norm, tm=512, chunk=256):
    n, d = x.shape
    d_ff = wd.shape[0]
    row = pl.BlockSpec((tm, d), lambda i: (i, 0))
    return pl.pallas_call(
        functools.partial(_ffn_kernel, d_ff=d_ff, chunk=chunk, final_norm=final_norm),
        grid=(n // tm,),
        in_specs=[row, _const_spec((1, d)), _const_spec(wgu.shape), _const_spec(wd.shape), _const_spec((1, d))],
        out_specs=row,
        out_shape=jax.ShapeDtypeStruct((n, d), F32),
        compiler_params=pltpu.CompilerParams(dimension_semantics=("arbitrary",), vmem_limit_bytes=VMEM_LIMIT),
        name="ffn_final" if final_norm else "ffn",
    )(x, gain, wgu, wd, final_gain)


def _inproj_kernel(x_ref, g_ref, w_ref, bg_ref, ct_ref, s1_ref, s2_ref,
                   q_ref, k_ref, vt_ref, c_ref, gate_ref, *, d, chunk):
    hn = _rms(x_ref[...], g_ref[...]).astype(BF16)
    ct, s1, s2 = ct_ref[...], s1_ref[...], s2_ref[...]
    tm = hn.shape[0]
    ones_rows = (lax.broadcasted_iota(jnp.int32, (BF16_ROWS, tm), 0) == 0).astype(BF16)

    def proj(lo):
        return jnp.dot(hn, w_ref[:, lo:lo + chunk], preferred_element_type=F32)

    def rope(z):
        blocks = []
        for b in range(chunk // LANES):
            zb = z[:, b * LANES:(b + 1) * LANES]
            up = pltpu.roll(zb, LANES - ROT_DIM // 2, 1)
            dn = pltpu.roll(zb, ROT_DIM // 2, 1)
            blocks.append(zb * ct + up * s1 + dn * s2)
        return jnp.concatenate(blocks, axis=-1)

    for c in range(d // chunk):
        lo = c * chunk
        q_ref[:, lo:lo + chunk] = (rope(proj(lo)) * Q_SCALE).astype(BF16)
        k_ref[:, lo:lo + chunk] = rope(proj(d + lo)).astype(BF16)
        vz = proj(2 * d + lo)
        for b in range(chunk // V_HEAD_DIM):
            h = (lo + b * V_HEAD_DIM) // V_HEAD_DIM
            vt_ref[0, h, :V_HEAD_DIM, :] = vz[:, b * V_HEAD_DIM:(b + 1) * V_HEAD_DIM].T.astype(BF16)
            vt_ref[0, h, V_HEAD_DIM:, :] = ones_rows
        c_ref[:, lo:lo + chunk] = proj(3 * d + lo) * jax.nn.sigmoid(proj(4 * d + lo))
    for c in range(2 * d // chunk):
        lo = c * chunk
        gate_ref[:, lo:lo + chunk] = jax.nn.sigmoid(proj(5 * d + lo) + bg_ref[:, lo:lo + chunk])


def _inproj(x, gain, w_in, b_gate, ct, s1, s2, *, seq, tm=512, chunk=512):
    n, d = x.shape
    row = pl.BlockSpec((tm, d), lambda i: (i, 0))
    tab = pl.BlockSpec((tm, LANES), lambda i: (i % (seq // tm), 0))
    return pl.pallas_call(
        functools.partial(_inproj_kernel, d=d, chunk=chunk),
        grid=(n // tm,),
        in_specs=[row, _const_spec((1, d)), _const_spec(w_in.shape), _const_spec((1, 2 * d)), tab, tab, tab],
        out_specs=[row, row, pl.BlockSpec((1, N_HEADS, VT_ROWS, tm), lambda i: (i, 0, 0, 0)),
                   row, pl.BlockSpec((tm, 2 * d), lambda i: (i, 0))],
        out_shape=[jax.ShapeDtypeStruct((n, d), BF16)] * 2
        + [jax.ShapeDtypeStruct((n // tm, N_HEADS, VT_ROWS, tm), BF16),
           jax.ShapeDtypeStruct((n, d), F32), jax.ShapeDtypeStruct((n, 2 * d), F32)],
        compiler_params=pltpu.CompilerParams(dimension_semantics=("arbitrary",), vmem_limit_bytes=VMEM_LIMIT),
        name="inproj",
    )(x, gain, w_in, b_gate, ct, s1, s2)


def _attn_kernel(lam_ref, g_ref, q_ref, k_ref, vt_ref, o_ref,
                 m_ref, acc_ref, sa_ref, sb_ref, xa_ref, xb_ref, *, tq, tk, hp, nq):
    def tile(i, tile_carry):
        tile_rows = pl.ds(pl.multiple_of(i * tq, tq), tq)
        lane = lax.broadcasted_iota(jnp.int32, (tq, V_HEAD_DIM), 1)
        qs = []
        for hh in range(hp):
            q = q_ref[tile_rows, hh * V_HEAD_DIM:(hh + 1) * V_HEAD_DIM]
            qs.append((jnp.where(lane < HEAD_DIM, q, jnp.zeros_like(q)), jnp.where(lane >= HEAD_DIM, q, jnp.zeros_like(q))))

        m_ref[...] = jnp.full(m_ref.shape, NEG_INF, F32)
        acc_ref[...] = jnp.zeros(acc_ref.shape, F32)

        def scores(j, s_ref, x_ref, hh, c):
            kc = k_ref[pl.ds(pl.multiple_of(j * tk, tk), tk), hh * V_HEAD_DIM:(hh + 1) * V_HEAD_DIM]
            s = lax.dot_general(kc, qs[hh][c], (((1,), (1,)), ((), ())), preferred_element_type=F32)
            s_ref[2 * hh + c] = s
            x_ref[2 * hh + c] = jnp.max(s, axis=0, keepdims=True)

        def consume(j, s_ref, x_ref, hh, c, masked):
            st = 2 * hh + c
            vt = vt_ref[j, hh]
            s = s_ref[st]
            if masked:
                key = lax.broadcasted_iota(jnp.int32, s.shape, 0)
                qry = lax.broadcasted_iota(jnp.int32, s.shape, 1)
                s = jnp.where(key <= qry, s, NEG_INF)
                blk_max = jnp.max(s, axis=0, keepdims=True)
            else:
                blk_max = x_ref[st]
            m_prev = m_ref[st]
            m_new = jnp.maximum(m_prev, blk_max)
            p = jnp.exp2(s - m_new).astype(BF16)
            alpha = jnp.exp2(m_prev - m_new)
            acc_ref[st] = alpha * acc_ref[st] + jnp.dot(vt, p, preferred_element_type=F32)
            m_ref[st] = m_new

        def phase(j_next, nxt, j_cur, cur, masked=False):
            for hh in range(hp):
                for c in range(2):
                    if nxt is not None:
                        scores(j_next, nxt[0], nxt[1], hh, c)
                    consume(j_cur, cur[0], cur[1], hh, c, masked)

        a_bufs, b_bufs = (sa_ref, xa_ref), (sb_ref, xb_ref)
        for hh in range(hp):
            for c in range(2):
                scores(0, sa_ref, xa_ref, hh, c)

        def pair(t, carry):
            phase(2 * t + 1, b_bufs, 2 * t, a_bufs)
            phase(2 * t + 2, a_bufs, 2 * t + 1, b_bufs)
            return carry

        def quad(u, carry):
            pair(2 * u, carry)
            pair(2 * u + 1, carry)
            return carry

        lax.fori_loop(0, i // 4, quad, 0)
        lax.fori_loop(2 * (i // 4), i // 2, pair, 0)

        @pl.when(i % 2 == 0)
        def _():
            phase(None, None, i, a_bufs, masked=True)

        @pl.when(i % 2 == 1)
        def _():
            phase(i, b_bufs, i - 1, a_bufs)
            phase(None, None, i, b_bufs, masked=True)

        lp = lam_ref[...]
        lam = (jnp.exp(jnp.sum(lp[0:1] * lp[1:2], axis=-1, keepdims=True))
               - jnp.exp(jnp.sum(lp[2:3] * lp[3:4], axis=-1, keepdims=True)) + LAMBDA_INIT)
        for hh in range(hp):
            a1, a2 = acc_ref[2 * hh], acc_ref[2 * hh + 1]
            inv1 = 1.0 / a1[V_HEAD_DIM:V_HEAD_DIM + 1]
            inv2 = 1.0 / a2[V_HEAD_DIM:V_HEAD_DIM + 1]
            ot = a1[:V_HEAD_DIM] * inv1 - lam * (a2[:V_HEAD_DIM] * inv2)
            ot = ot * lax.rsqrt(jnp.mean(ot * ot, axis=0, keepdims=True) + NORM_EPS)
            o_ref[tile_rows, hh * V_HEAD_DIM:(hh + 1) * V_HEAD_DIM] = (ot.T * g_ref[...] * (1.0 - LAMBDA_INIT)).astype(BF16)
        return tile_carry

    lax.fori_loop(0, nq, tile, 0)


def _attention(lam_params, subln, q, k, vt, *, tq=512, hp=2):
    b, s, d = q.shape
    tk = vt.shape[-1]
    assert tq == tk
    w = hp * V_HEAD_DIM
    qo = pl.BlockSpec((None, s, w), lambda bi, g: (bi, 0, g))
    kspec = qo
    vspec = pl.BlockSpec((s // tk, hp, VT_ROWS, tk), lambda bi, g: (bi, g, 0, 0))
    streams = 2 * hp
    return pl.pallas_call(
        functools.partial(_attn_kernel, tq=tq, tk=tk, hp=hp, nq=s // tq),
        grid=(b, N_HEADS // hp),
        in_specs=[_const_spec(lam_params.shape), _const_spec((1, V_HEAD_DIM)), qo, kspec, vspec],
        out_specs=qo,
        out_shape=jax.ShapeDtypeStruct((b, s, d), BF16),
        scratch_shapes=[pltpu.VMEM((streams, 1, tq), F32), pltpu.VMEM((streams, VT_ROWS, tq), F32),
                        pltpu.VMEM((streams, tk, tq), F32), pltpu.VMEM((streams, tk, tq), F32),
                        pltpu.VMEM((streams, 1, tq), F32), pltpu.VMEM((streams, 1, tq), F32)],
        compiler_params=pltpu.CompilerParams(dimension_semantics=("arbitrary",) * 2, vmem_limit_bytes=VMEM_LIMIT),
        name="diff_attn",
    )(lam_params, subln, q, k, vt)


def _merge_kernel(x_ref, o_ref, c_ref, halo_ref, gate_ref, cw_ref, cb_ref, lg_ref, lb_ref,
                  wa_ref, wb_ref, wo_ref, out_ref, buf_ref, cv_ref, *, tm, d, tiles_per_seq, rows):
    y_a = jnp.dot(o_ref[...], wa_ref[...], preferred_element_type=F32)
    first = pl.program_id(0) % tiles_per_seq == 0
    halo = halo_ref[...]
    buf_ref[0:CONV_HALO, :] = jnp.where(first, jnp.zeros_like(halo), halo)
    buf_ref[CONV_HALO:, :] = c_ref[...]

    groups = rows // SUBLANES
    for r0 in range(0, tm, rows):
        for lb in range(d // LANES):
            cols = slice(lb * LANES, (lb + 1) * LANES)
            win = buf_ref[r0:r0 + rows + CONV_HALO, cols]
            acc = jnp.broadcast_to(cb_ref[:, cols], (groups, SUBLANES, LANES))
            for r in range(SUBLANES):
                wr = win if r == 0 else pltpu.roll(win, r, 0)
                for a in range(CONV_HALO // SUBLANES):
                    shift = SUBLANES * a + r
                    if shift < CONV_WIDTH:
                        j = CONV_WIDTH - 1 - shift
                        lo = CONV_HALO - SUBLANES * a
                        w8 = jnp.broadcast_to(cw_ref[j:j + 1, cols], (SUBLANES, LANES))
                        acc = acc + w8[None] * wr[lo:lo + rows].reshape(groups, SUBLANES, LANES)
            cv_ref[r0:r0 + rows, cols] = acc.reshape(rows, LANES)

    cv = cv_ref[...]
    mu = jnp.mean(cv, axis=-1, keepdims=True)
    cen = cv - mu
    var = jnp.mean(cen * cen, axis=-1, keepdims=True)
    y = cen * lax.rsqrt(var + NORM_EPS) * lg_ref[...] + lb_ref[...]
    y = (y * jax.nn.sigmoid(y)).astype(BF16)
    y_b = jnp.dot(y, wb_ref[...], preferred_element_type=F32)
    mix = (gate_ref[:, :d] * y_a + gate_ref[:, d:] * y_b).astype(BF16)
    out_ref[...] = x_ref[...] + jnp.dot(mix, wo_ref[...], preferred_element_type=F32)


def _merge(x, o, c, gates, conv_w, conv_b, ln_g, ln_b, wa, wb, wo, *, seq, tm=512, rows=64):
    n, d = x.shape
    row = pl.BlockSpec((tm, d), lambda i: (i, 0))
    halo = pl.BlockSpec((CONV_HALO, d), lambda i: (jnp.maximum(i * (tm // CONV_HALO) - 1, 0), 0))
    vec = _const_spec((1, d))
    mat = _const_spec((d, d))
    return pl.pallas_call(
        functools.partial(_merge_kernel, tm=tm, d=d, tiles_per_seq=seq // tm, rows=rows),
        grid=(n // tm,),
        in_specs=[row, row, row, halo, pl.BlockSpec((tm, 2 * d), lambda i: (i, 0)),
                  _const_spec(conv_w.shape), vec, vec, vec, mat, mat, mat],
        out_specs=row,
        out_shape=jax.ShapeDtypeStruct((n, d), F32),
        scratch_shapes=[pltpu.VMEM((CONV_HALO + tm, d), F32), pltpu.VMEM((tm, d), F32)],
        compiler_params=pltpu.CompilerParams(dimension_semantics=("arbitrary",), vmem_limit_bytes=VMEM_LIMIT),
        name="conv_merge",
    )(x, o, c, c, gates, conv_w, conv_b, ln_g, ln_b, wa, wb, wo)


def _rope_tables(seq):
    pos = jnp.arange(seq, dtype=F32)
    inv_freq = ROPE_THETA ** (-jnp.arange(0, ROT_DIM, 2, dtype=F32) / ROT_DIM)
    ang = pos[:, None] * inv_freq[None, :]
    cos, sin = jnp.cos(ang), jnp.sin(ang)
    half = ROT_DIM // 2
    pad = HEAD_DIM - ROT_DIM
    ones = jnp.ones((seq, pad), F32)
    zeros_h = jnp.zeros((seq, half), F32)
    zeros_p = jnp.zeros((seq, pad), F32)
    ct = jnp.concatenate([cos, cos, ones], axis=-1)
    s1 = jnp.concatenate([-sin, zeros_h, zeros_p], axis=-1)
    s2 = jnp.concatenate([zeros_h, sin, zeros_p], axis=-1)
    rep = LANES // HEAD_DIM
    return tuple(jnp.tile(t, (1, rep)) for t in (ct, s1, s2))


def kernel(x, ffn1_norm, ffn1_w_gate_up, ffn1_w_down, mix_norm, w_in, b_gate, lambda_q1, lambda_k1, lambda_q2, lambda_k2, attn_subln, w_attn_out, conv_w, conv_b, conv_ln_g, conv_ln_b, w_conv_out, w_out, ffn2_norm, ffn2_w_gate_up, ffn2_w_down, final_norm):
    bsz, seq, d = x.shape
    n = bsz * seq
    l = 0
    xf = x.reshape(n, d)
    ct, s1, s2 = _rope_tables(seq)
    final_gain = final_norm.reshape(1, d)

    x1 = _ffn(xf, ffn1_norm[l].reshape(1, d), ffn1_w_gate_up[l].astype(BF16), ffn1_w_down[l].astype(BF16),
              final_gain, final_norm=False)
    q, k, vt, c, gates = _inproj(x1, mix_norm[l].reshape(1, d), w_in[l].astype(BF16), b_gate[l].reshape(1, 2 * d),
                                ct, s1, s2, seq=seq)
    lam_params = jnp.stack([lambda_q1[l], lambda_k1[l], lambda_q2[l], lambda_k2[l]]).astype(F32)
    o = _attention(lam_params, attn_subln[l].reshape(1, V_HEAD_DIM),
                   q.reshape(bsz, seq, d), k.reshape(bsz, seq, d), vt)
    x2 = _merge(x1, o.reshape(n, d), c, gates, conv_w[l], conv_b[l].reshape(1, d),
                conv_ln_g[l].reshape(1, d), conv_ln_b[l].reshape(1, d),
                w_attn_out[l].astype(BF16), w_conv_out[l].astype(BF16), w_out[l].astype(BF16), seq=seq)
    x3 = _ffn(x2, ffn2_norm[l].reshape(1, d), ffn2_w_gate_up[l].astype(BF16), ffn2_w_down[l].astype(BF16),
              final_gain, final_norm=True)
    return x3.reshape(bsz, seq, d)
```

```python
import functools
import math

import jax
import jax.numpy as jnp
from jax import lax
from jax.experimental import pallas as pl
from jax.experimental.pallas import tpu as pltpu

N_HEADS = 8
HEAD_DIM = 64
V_HEAD_DIM = 2 * HEAD_DIM
ROPE_THETA = 500000.0
ROT_DIM = HEAD_DIM // 4
CONV_WIDTH = 31
NORM_EPS = 1e-5
NEG_INF = -1e30
LAMBDA_INIT = 0.8 - 0.6 * math.exp(-0.3 * 0)
Q_SCALE = HEAD_DIM ** -0.5 * math.log2(math.e)

LANES = 128
SUBLANES = 8
BF16_ROWS = 16
VT_ROWS = 2 * 64 + BF16_ROWS
CONV_HALO = 32
VMEM_LIMIT = 56 * 1024 * 1024

BF16 = jnp.bfloat16
F32 = jnp.float32


def _rms(x, gain):
    return x * lax.rsqrt(jnp.mean(x * x, axis=-1, keepdims=True) + NORM_EPS) * gain


def _const_spec(shape):
    return pl.BlockSpec(shape, lambda *_: (0,) * len(shape))


def _ffn_kernel(x_ref, g_ref, wgu_ref, wd_ref, fg_ref, o_ref, *, d_ff, chunk, final_norm):
    x = x_ref[...]
    xn = _rms(x, g_ref[...]).astype(BF16)
    y = jnp.zeros(x.shape, F32)
    for c in range(d_ff // chunk):
        a = jnp.dot(xn, wgu_ref[:, c * chunk:(c + 1) * chunk], preferred_element_type=F32)
        b = jnp.dot(xn, wgu_ref[:, d_ff + c * chunk:d_ff + (c + 1) * chunk], preferred_element_type=F32)
        act = (a * jax.nn.sigmoid(a) * b).astype(BF16)
        y = y + jnp.dot(act, wd_ref[c * chunk:(c + 1) * chunk, :], preferred_element_type=F32)
    out = x + 0.5 * y
    if final_norm:
        out = _rms(out, fg_ref[...])
    o_ref[...] = out


def _ffn(x, gain, wgu, wd, final_gain, *, final_norm, tm=512, chunk=256):
    n, d = x.shape
    d_ff = wd.shape[0]
    row = pl.BlockSpec((tm, d), lambda i: (i, 0))
    return pl.pallas_call(
        functools.partial(_ffn_kernel, d_ff=d_ff, chunk=chunk, final_norm=final_norm),
        grid=(n // tm,),
        in_specs=[row, _const_spec((1, d)), _const_spec(wgu.shape), _const_spec(wd.shape), _const_spec((1, d))],
        out_specs=row,
        out_shape=jax.ShapeDtypeStruct((n, d), F32),
        compiler_params=pltpu.CompilerParams(dimension_semantics=("arbitrary",), vmem_limit_bytes=VMEM_LIMIT),
        name="ffn_final" if final_norm else "ffn",
    )(x, gain, wgu, wd, final_gain)


def _inproj_kernel(x_ref, g_ref, w_ref, bg_ref, ct_ref, s1_ref, s2_ref,
                   q_ref, k_ref, vt_ref, c_ref, gate_ref, *, d, chunk):
    hn = _rms(x_ref[...], g_ref[...]).astype(BF16)
    ct, s1, s2 = ct_ref[...], s1_ref[...], s2_ref[...]
    tm = hn.shape[0]
    ones_rows = (lax.broadcasted_iota(jnp.int32, (BF16_ROWS, tm), 0) == 0).astype(BF16)

    def proj(lo):
        return jnp.dot(hn, w_ref[:, lo:lo + chunk], preferred_element_type=F32)

    def rope(z):
        blocks = []
        for b in range(chunk // LANES):
            zb = z[:, b * LANES:(b + 1) * LANES]
            up = pltpu.roll(zb, LANES - ROT_DIM // 2, 1)
            dn = pltpu.roll(zb, ROT_DIM // 2, 1)
            blocks.append(zb * ct + up * s1 + dn * s2)
        return jnp.concatenate(blocks, axis=-1)

    for c in range(d // chunk):
        lo = c * chunk
        q_ref[:, lo:lo + chunk] = (rope(proj(lo)) * Q_SCALE).astype(BF16)
        k_ref[:, lo:lo + chunk] = rope(proj(d + lo)).astype(BF16)
        vz = proj(2 * d + lo)
        for b in range(chunk // V_HEAD_DIM):
            h = (lo + b * V_HEAD_DIM) // V_HEAD_DIM
            vt_ref[0, h, :V_HEAD_DIM, :] = vz[:, b * V_HEAD_DIM:(b + 1) * V_HEAD_DIM].T.astype(BF16)
            vt_ref[0, h, V_HEAD_DIM:, :] = ones_rows
        c_ref[:, lo:lo + chunk] = proj(3 * d + lo) * jax.nn.sigmoid(proj(4 * d + lo))
    for c in range(2 * d // chunk):
        lo = c * chunk
        gate_ref[:, lo:lo + chunk] = jax.nn.sigmoid(proj(5 * d + lo) + bg_ref[:, lo:lo + chunk])


def _inproj(x, gain, w_in, b_gate, ct, s1, s2, *, seq, tm=512, chunk=512):
    n, d = x.shape
    row = pl.BlockSpec((tm, d), lambda i: (i, 0))
    tab = pl.BlockSpec((tm, LANES), lambda i: (i % (seq // tm), 0))
    return pl.pallas_call(
        functools.partial(_inproj_kernel, d=d, chunk=chunk),
        grid=(n // tm,),
        in_specs=[row, _const_spec((1, d)), _const_spec(w_in.shape), _const_spec((1, 2 * d)), tab, tab, tab],
        out_specs=[row, row, pl.BlockSpec((1, N_HEADS, VT_ROWS, tm), lambda i: (i, 0, 0, 0)),
                   row, pl.BlockSpec((tm, 2 * d), lambda i: (i, 0))],
        out_shape=[jax.ShapeDtypeStruct((n, d), BF16)] * 2
        + [jax.ShapeDtypeStruct((n // tm, N_HEADS, VT_ROWS, tm), BF16),
           jax.ShapeDtypeStruct((n, d), F32), jax.ShapeDtypeStruct((n, 2 * d), F32)],
        compiler_params=pltpu.CompilerParams(dimension_semantics=("arbitrary",), vmem_limit_bytes=VMEM_LIMIT),
        name="inproj",
    )(x, gain, w_in, b_gate, ct, s1, s2)


def _attn_kernel(lam_ref, g_ref, q_ref, k_ref, vt_ref, o_ref,
                 m_ref, acc_ref, sa_ref, sb_ref, sc_ref, xa_ref, xb_ref, xc_ref, *, tq, tk, hp, nq):
    a_bufs, b_bufs, c_bufs = (sa_ref, xa_ref), (sb_ref, xb_ref), (sc_ref, xc_ref)
    lane = lax.broadcasted_iota(jnp.int32, (tq, V_HEAD_DIM), 1)

    def masked_queries(i):
        rows = pl.ds(pl.multiple_of(i * tq, tq), tq)
        qs = []
        for hh in range(hp):
            q = q_ref[rows, hh * V_HEAD_DIM:(hh + 1) * V_HEAD_DIM]
            qs.append((jnp.where(lane < HEAD_DIM, q, jnp.zeros_like(q)), jnp.where(lane >= HEAD_DIM, q, jnp.zeros_like(q))))
        return qs

    def scores(j, bufs, qs, hh, c):
        kc = k_ref[pl.ds(pl.multiple_of(j * tk, tk), tk), hh * V_HEAD_DIM:(hh + 1) * V_HEAD_DIM]
        s = lax.dot_general(kc, qs[hh][c], (((1,), (1,)), ((), ())), preferred_element_type=F32)
        bufs[0][2 * hh + c] = s
        bufs[1][2 * hh + c] = jnp.max(s, axis=0, keepdims=True)

    def consume(j, bufs, hh, c, masked):
        st = 2 * hh + c
        vt = vt_ref[j, hh]
        s = bufs[0][st]
        if masked:
            key = lax.broadcasted_iota(jnp.int32, s.shape, 0)
            qry = lax.broadcasted_iota(jnp.int32, s.shape, 1)
            s = jnp.where(key <= qry, s, NEG_INF)
            blk_max = jnp.max(s, axis=0, keepdims=True)
        else:
            blk_max = bufs[1][st]
        m_prev = m_ref[st]
        m_new = jnp.maximum(m_prev, blk_max)
        p = jnp.exp2(s - m_new).astype(BF16)
        alpha = jnp.exp2(m_prev - m_new)
        acc_ref[st] = alpha * acc_ref[st] + jnp.dot(vt, p, preferred_element_type=F32)
        m_ref[st] = m_new

    def phase(j_next, nxt, qs_next, j_cur, cur, masked=False):
        for hh in range(hp):
            for c in range(2):
                scores(j_next, nxt, qs_next, hh, c)
                consume(j_cur, cur, hh, c, masked)

    def reset_state():
        m_ref[...] = jnp.full(m_ref.shape, NEG_INF, F32)
        acc_ref[...] = jnp.zeros(acc_ref.shape, F32)

    def finalize(i):
        rows = pl.ds(pl.multiple_of(i * tq, tq), tq)
        lp = lam_ref[...]
        lam = (jnp.exp(jnp.sum(lp[0:1] * lp[1:2], axis=-1, keepdims=True))
               - jnp.exp(jnp.sum(lp[2:3] * lp[3:4], axis=-1, keepdims=True)) + LAMBDA_INIT)
        for hh in range(hp):
            a1, a2 = acc_ref[2 * hh], acc_ref[2 * hh + 1]
            inv1 = 1.0 / a1[V_HEAD_DIM:V_HEAD_DIM + 1]
            inv2 = 1.0 / a2[V_HEAD_DIM:V_HEAD_DIM + 1]
            ot = a1[:V_HEAD_DIM] * inv1 - lam * (a2[:V_HEAD_DIM] * inv2)
            ot = ot * lax.rsqrt(jnp.mean(ot * ot, axis=0, keepdims=True) + NORM_EPS)
            o_ref[rows, hh * V_HEAD_DIM:(hh + 1) * V_HEAD_DIM] = (ot.T * g_ref[...] * (1.0 - LAMBDA_INIT)).astype(BF16)

    qs0 = masked_queries(0)
    for hh in range(hp):
        for c in range(2):
            scores(0, a_bufs, qs0, hh, c)
    reset_state()
    phase(0, c_bufs, masked_queries(min(1, nq - 1)), 0, a_bufs, masked=True)
    finalize(0)

    def tile(i, tile_carry):
        qs = masked_queries(i)
        qs_next = masked_queries(jnp.minimum(i + 1, nq - 1))
        reset_state()
        phase(1, a_bufs, qs, 0, c_bufs)

        def pair(t, carry):
            phase(2 * t + 2, b_bufs, qs, 2 * t + 1, a_bufs)
            phase(2 * t + 3, a_bufs, qs, 2 * t + 2, b_bufs)
            return carry

        def quad(u, carry):
            pair(2 * u, carry)
            pair(2 * u + 1, carry)
            return carry

        def octet(v, carry):
            quad(2 * v, carry)
            quad(2 * v + 1, carry)
            return carry

        n_pairs = (i - 1) // 2
        n_quads = n_pairs // 2
        lax.fori_loop(0, n_quads // 2, octet, 0)
        lax.fori_loop(2 * (n_quads // 2), n_quads, quad, 0)
        lax.fori_loop(2 * n_quads, n_pairs, pair, 0)

        @pl.when(i % 2 == 0)
        def _():
            phase(i, b_bufs, qs, i - 1, a_bufs)
            phase(0, c_bufs, qs_next, i, b_bufs, masked=True)

        @pl.when(i % 2 == 1)
        def _():
            phase(0, c_bufs, qs_next, i, a_bufs, masked=True)

        finalize(i)
        return tile_carry

    lax.fori_loop(1, nq, tile, 0)


def _attention(lam_params, subln, q, k, vt, *, tq=512, hp=2):
    b, s, d = q.shape
    tk = vt.shape[-1]
    assert tq == tk
    w = hp * V_HEAD_DIM
    qo = pl.BlockSpec((None, s, w), lambda bi, g: (bi, 0, g))
    kspec = qo
    vspec = pl.BlockSpec((s // tk, hp, VT_ROWS, tk), lambda bi, g: (bi, g, 0, 0))
    streams = 2 * hp
    return pl.pallas_call(
        functools.partial(_attn_kernel, tq=tq, tk=tk, hp=hp, nq=s // tq),
        grid=(b, N_HEADS // hp),
        in_specs=[_const_spec(lam_params.shape), _const_spec((1, V_HEAD_DIM)), qo, kspec, vspec],
        out_specs=qo,
        out_shape=jax.ShapeDtypeStruct((b, s, d), BF16),
        scratch_shapes=[pltpu.VMEM((streams, 1, tq), F32), pltpu.VMEM((streams, VT_ROWS, tq), F32),
                        pltpu.VMEM((streams, tk, tq), F32), pltpu.VMEM((streams, tk, tq), F32),
                        pltpu.VMEM((streams, tk, tq), F32),
                        pltpu.VMEM((streams, 1, tq), F32), pltpu.VMEM((streams, 1, tq), F32),
                        pltpu.VMEM((streams, 1, tq), F32)],
        compiler_params=pltpu.CompilerParams(dimension_semantics=("arbitrary",) * 2, vmem_limit_bytes=VMEM_LIMIT),
        name="diff_attn",
    )(lam_params, subln, q, k, vt)


def _merge_kernel(x_ref, o_ref, c_ref, halo_ref, gate_ref, cw_ref, cb_ref, lg_ref, lb_ref,
                  wa_ref, wb_ref, wo_ref, out_ref, buf_ref, cv_ref, *, tm, d, tiles_per_seq, rows):
    y_a = jnp.dot(o_ref[...], wa_ref[...], preferred_element_type=F32)
    first = pl.program_id(0) % tiles_per_seq == 0
    halo = halo_ref[...]
    buf_ref[0:CONV_HALO, :] = jnp.where(first, jnp.zeros_like(halo), halo)
    buf_ref[CONV_HALO:, :] = c_ref[...]

    groups = rows // SUBLANES
    for r0 in range(0, tm, rows):
        for lb in range(d // LANES):
            cols = slice(lb * LANES, (lb + 1) * LANES)
            win = buf_ref[r0:r0 + rows + CONV_HALO, cols]
            acc = jnp.broadcast_to(cb_ref[:, cols], (groups, SUBLANES, LANES))
            for r in range(SUBLANES):
                wr = win if r == 0 else pltpu.roll(win, r, 0)
                for a in range(CONV_HALO // SUBLANES):
                    shift = SUBLANES * a + r
                    if shift < CONV_WIDTH:
                        j = CONV_WIDTH - 1 - shift
                        lo = CONV_HALO - SUBLANES * a
                        w8 = jnp.broadcast_to(cw_ref[j:j + 1, cols], (SUBLANES, LANES))
                        acc = acc + w8[None] * wr[lo:lo + rows].reshape(groups, SUBLANES, LANES)
            cv_ref[r0:r0 + rows, cols] = acc.reshape(rows, LANES)

    cv = cv_ref[...]
    mu = jnp.mean(cv, axis=-1, keepdims=True)
    cen = cv - mu
    var = jnp.mean(cen * cen, axis=-1, keepdims=True)
    y = cen * lax.rsqrt(var + NORM_EPS) * lg_ref[...] + lb_ref[...]
    y = (y * jax.nn.sigmoid(y)).astype(BF16)
    y_b = jnp.dot(y, wb_ref[...], preferred_element_type=F32)
    mix = (gate_ref[:, :d] * y_a + gate_ref[:, d:] * y_b).astype(BF16)
    out_ref[...] = x_ref[...] + jnp.dot(mix, wo_ref[...], preferred_element_type=F32)


def _merge(x, o, c, gates, conv_w, conv_b, ln_g, ln_b, wa, wb, wo, *, seq, tm=512, rows=64):
    n, d = x.shape
    row = pl.BlockSpec((tm, d), lambda i: (i, 0))
    halo = pl.BlockSpec((CONV_HALO, d), lambda i: (jnp.maximum(i * (tm // CONV_HALO) - 1, 0), 0))
    vec = _const_spec((1, d))
    mat = _const_spec((d, d))
    return pl.pallas_call(
        functools.partial(_merge_kernel, tm=tm, d=d, tiles_per_seq=seq // tm, rows=rows),
        grid=(n // tm,),
        in_specs=[row, row, row, halo, pl.BlockSpec((tm, 2 * d), lambda i: (i, 0)),
                  _const_spec(conv_w.shape), vec, vec, vec, mat, mat, mat],
        out_specs=row,
        out_shape=jax.ShapeDtypeStruct((n, d), F32),
        scratch_shapes=[pltpu.VMEM((CONV_HALO + tm, d), F32), pltpu.VMEM((tm, d), F32)],
        compiler_params=pltpu.CompilerParams(dimension_semantics=("arbitrary",), vmem_limit_bytes=VMEM_LIMIT),
        name="conv_merge",
    )(x, o, c, c, gates, conv_w, conv_b, ln_g, ln_b, wa, wb, wo)


def _rope_tables(seq):
    pos = jnp.arange(seq, dtype=F32)
    inv_freq = ROPE_THETA ** (-jnp.arange(0, ROT_DIM, 2, dtype=F32) / ROT_DIM)
    ang = pos[:, None] * inv_freq[None, :]
    cos, sin = jnp.cos(ang), jnp.sin(ang)
    half = ROT_DIM // 2
    pad = HEAD_DIM - ROT_DIM
    ones = jnp.ones((seq, pad), F32)
    zeros_h = jnp.zeros((seq, half), F32)
    zeros_p = jnp.zeros((seq, pad), F32)
    ct = jnp.concatenate([cos, cos, ones], axis=-1)
    s1 = jnp.concatenate([-sin, zeros_h, zeros_p], axis=-1)
    s2 = jnp.concatenate([zeros_h, sin, zeros_p], axis=-1)
    rep = LANES // HEAD_DIM
    return tuple(jnp.tile(t, (1, rep)) for t in (ct, s1, s2))


def kernel(x, ffn1_norm, ffn1_w_gate_up, ffn1_w_down, mix_norm, w_in, b_gate, lambda_q1, lambda_k1, lambda_q2, lambda_k2, attn_subln, w_attn_out, conv_w, conv_b, conv_ln_g, conv_ln_b, w_conv_out, w_out, ffn2_norm, ffn2_w_gate_up, ffn2_w_down, final_norm):
    bsz, seq, d = x.shape
    n = bsz * seq
    l = 0
    xf = x.reshape(n, d)
    ct, s1, s2 = _rope_tables(seq)
    final_gain = final_norm.reshape(1, d)

    x1 = _ffn(xf, ffn1_norm[l].reshape(1, d), ffn1_w_gate_up[l].astype(BF16), ffn1_w_down[l].astype(BF16),
              final_gain, final_norm=False)
    q, k, vt, c, gates = _inproj(x1, mix_norm[l].reshape(1, d), w_in[l].astype(BF16), b_gate[l].reshape(1, 2 * d),
                                ct, s1, s2, seq=seq)
    lam_params = jnp.stack([lambda_q1[l], lambda_k1[l], lambda_q2[l], lambda_k2[l]]).astype(F32)
    o = _attention(lam_params, attn_subln[l].reshape(1, V_HEAD_DIM),
                   q.reshape(bsz, seq, d), k.reshape(bsz, seq, d), vt)
    x2 = _merge(x1, o.reshape(n, d), c, gates, conv_w[l], conv_b[l].reshape(1, d),
                conv_ln_g[l].reshape(1, d), conv_ln_b[l].reshape(1, d),
                w_attn_out[l].astype(BF16), w_conv_out[l].astype(BF16), w_out[l].astype(BF16), seq=seq)
    x3 = _ffn(x2, ffn2_norm[l].reshape(1, d), ffn2_w_gate_up[l].astype(BF16), ffn2_w_down[l].astype(BF16),
              final_gain, final_norm=True)
    return x3.reshape(bsz, seq, d)
```

```python
import functools
import math

import jax
import jax.numpy as jnp
from jax import lax
from jax.experimental import pallas as pl
from jax.experimental.pallas import tpu as pltpu

N_HEADS = 8
HEAD_DIM = 64
V_HEAD_DIM = 2 * HEAD_DIM
ROPE_THETA = 500000.0
ROT_DIM = HEAD_DIM // 4
CONV_WIDTH = 31
NORM_EPS = 1e-5
NEG_INF = -1e30
LAMBDA_INIT = 0.8 - 0.6 * math.exp(-0.3 * 0)
Q_SCALE = HEAD_DIM ** -0.5 * math.log2(math.e)

LANES = 128
SUBLANES = 8
BF16_ROWS = 16
VT_ROWS = 2 * 64 + BF16_ROWS
CONV_HALO = 32
VMEM_LIMIT = 56 * 1024 * 1024

BF16 = jnp.bfloat16
F32 = jnp.float32


def _rms(x, gain):
    return x * lax.rsqrt(jnp.mean(x * x, axis=-1, keepdims=True) + NORM_EPS) * gain


def _const_spec(shape):
    return pl.BlockSpec(shape, lambda *_: (0,) * len(shape), pipeline_mode=pl.Buffered(1))


def _ffn_kernel(x_ref, g_ref, wgu_ref, wd_ref, fg_ref, o_ref, *, d_ff, chunk, final_norm):
    x = x_ref[...]
    xn = _rms(x, g_ref[...]).astype(BF16)
    y = jnp.zeros(x.shape, F32)
    for c in range(d_ff // chunk):
        a = jnp.dot(xn, wgu_ref[:, c * chunk:(c + 1) * chunk], preferred_element_type=F32)
        b = jnp.dot(xn, wgu_ref[:, d_ff + c * chunk:d_ff + (c + 1) * chunk], preferred_element_type=F32)
        act = (a * jax.nn.sigmoid(a) * b).astype(BF16)
        y = y + jnp.dot(act, wd_ref[c * chunk:(c + 1) * chunk, :], preferred_element_type=F32)
    out = x + 0.5 * y
    if final_norm:
        out = _rms(out, fg_ref[...])
    o_ref[...] = out


def _ffn(x, gain, wgu, wd, final_gain, *, final_norm, tm=1024, chunk=256):
    n, d = x.shape
    d_ff = wd.shape[0]
    row = pl.BlockSpec((tm, d), lambda i: (i, 0))
    return pl.pallas_call(
        functools.partial(_ffn_kernel, d_ff=d_ff, chunk=chunk, final_norm=final_norm),
        grid=(n // tm,),
        in_specs=[row, _const_spec((1, d)), _const_spec(wgu.shape), _const_spec(wd.shape), _const_spec((1, d))],
        out_specs=row,
        out_shape=jax.ShapeDtypeStruct((n, d), F32),
        compiler_params=pltpu.CompilerParams(dimension_semantics=("arbitrary",), vmem_limit_bytes=VMEM_LIMIT),
        name="ffn_final" if final_norm else "ffn",
    )(x, gain, wgu, wd, final_gain)


def _inproj_kernel(x_ref, g_ref, w_ref, bg_ref, ct_ref, s1_ref, s2_ref,
                   q_ref, k_ref, vt_ref, c_ref, gate_ref, *, d, chunk):
    hn = _rms(x_ref[...], g_ref[...]).astype(BF16)
    ct, s1, s2 = ct_ref[...], s1_ref[...], s2_ref[...]
    tm = hn.shape[0]
    ones_rows = (lax.broadcasted_iota(jnp.int32, (BF16_ROWS, tm), 0) == 0).astype(BF16)

    def proj(lo):
        return jnp.dot(hn, w_ref[:, lo:lo + chunk], preferred_element_type=F32)

    def rope(z):
        blocks = []
        for b in range(chunk // LANES):
            zb = z[:, b * LANES:(b + 1) * LANES]
            up = pltpu.roll(zb, LANES - ROT_DIM // 2, 1)
            dn = pltpu.roll(zb, ROT_DIM // 2, 1)
            blocks.append(zb * ct + up * s1 + dn * s2)
        return jnp.concatenate(blocks, axis=-1)

    for c in range(d // chunk):
        lo = c * chunk
        q_ref[:, lo:lo + chunk] = (rope(proj(lo)) * Q_SCALE).astype(BF16)
        k_ref[:, lo:lo + chunk] = rope(proj(d + lo)).astype(BF16)
        vz = proj(2 * d + lo)
        for b in range(chunk // V_HEAD_DIM):
            h = (lo + b * V_HEAD_DIM) // V_HEAD_DIM
            vt_ref[0, h, :V_HEAD_DIM, :] = vz[:, b * V_HEAD_DIM:(b + 1) * V_HEAD_DIM].T.astype(BF16)
            vt_ref[0, h, V_HEAD_DIM:, :] = ones_rows
        c_ref[:, lo:lo + chunk] = proj(3 * d + lo) * jax.nn.sigmoid(proj(4 * d + lo))
    for c in range(2 * d // chunk):
        lo = c * chunk
        gate_ref[:, lo:lo + chunk] = jax.nn.sigmoid(proj(5 * d + lo) + bg_ref[:, lo:lo + chunk])


def _inproj(x, gain, w_in, b_gate, ct, s1, s2, *, seq, tm=512, chunk=512):
    n, d = x.shape
    row = pl.BlockSpec((tm, d), lambda i: (i, 0))
    tab = pl.BlockSpec((tm, LANES), lambda i: (i % (seq // tm), 0))
    return pl.pallas_call(
        functools.partial(_inproj_kernel, d=d, chunk=chunk),
        grid=(n // tm,),
        in_specs=[row, _const_spec((1, d)), _const_spec(w_in.shape), _const_spec((1, 2 * d)), tab, tab, tab],
        out_specs=[row, row, pl.BlockSpec((1, N_HEADS, VT_ROWS, tm), lambda i: (i, 0, 0, 0)),
                   row, pl.BlockSpec((tm, 2 * d), lambda i: (i, 0))],
        out_shape=[jax.ShapeDtypeStruct((n, d), BF16)] * 2
        + [jax.ShapeDtypeStruct((n // tm, N_HEADS, VT_ROWS, tm), BF16),
           jax.ShapeDtypeStruct((n, d), F32), jax.ShapeDtypeStruct((n, 2 * d), F32)],
        compiler_params=pltpu.CompilerParams(dimension_semantics=("arbitrary",), vmem_limit_bytes=VMEM_LIMIT),
        name="inproj",
    )(x, gain, w_in, b_gate, ct, s1, s2)


def _attn_kernel(lam_ref, g_ref, q_ref, k_ref, vt_ref, o_ref,
                 m_ref, acc_ref, sa_ref, sb_ref, sc_ref, xa_ref, xb_ref, xc_ref, *, tq, tk, hp, nq):
    a_bufs, b_bufs, c_bufs = (sa_ref, xa_ref), (sb_ref, xb_ref), (sc_ref, xc_ref)
    lane = lax.broadcasted_iota(jnp.int32, (tq, V_HEAD_DIM), 1)

    def masked_queries(i):
        rows = pl.ds(pl.multiple_of(i * tq, tq), tq)
        qs = []
        for hh in range(hp):
            q = q_ref[rows, hh * V_HEAD_DIM:(hh + 1) * V_HEAD_DIM]
            qs.append((jnp.where(lane < HEAD_DIM, q, jnp.zeros_like(q)), jnp.where(lane >= HEAD_DIM, q, jnp.zeros_like(q))))
        return qs

    def scores(j, bufs, qs, hh, c):
        kc = k_ref[pl.ds(pl.multiple_of(j * tk, tk), tk), hh * V_HEAD_DIM:(hh + 1) * V_HEAD_DIM]
        s = lax.dot_general(kc, qs[hh][c], (((1,), (1,)), ((), ())), preferred_element_type=F32)
        bufs[0][2 * hh + c] = s
        bufs[1][2 * hh + c] = jnp.max(s, axis=0, keepdims=True)

    def consume(j, bufs, hh, c, masked):
        st = 2 * hh + c
        vt = vt_ref[j, hh]
        s = bufs[0][st]
        if masked:
            key = lax.broadcasted_iota(jnp.int32, s.shape, 0)
            qry = lax.broadcasted_iota(jnp.int32, s.shape, 1)
            s = jnp.where(key <= qry, s, NEG_INF)
            blk_max = jnp.max(s, axis=0, keepdims=True)
        else:
            blk_max = bufs[1][st]
        m_prev = m_ref[st]
        m_new = jnp.maximum(m_prev, blk_max)
        p = jnp.exp2(s - m_new).astype(BF16)
        alpha = jnp.exp2(m_prev - m_new)
        acc_ref[st] = alpha * acc_ref[st] + jnp.dot(vt, p, preferred_element_type=F32)
        m_ref[st] = m_new

    def phase(j_next, nxt, qs_next, j_cur, cur, masked=False):
        for hh in range(hp):
            for c in range(2):
                scores(j_next, nxt, qs_next, hh, c)
                consume(j_cur, cur, hh, c, masked)

    def reset_state():
        m_ref[...] = jnp.full(m_ref.shape, NEG_INF, F32)
        acc_ref[...] = jnp.zeros(acc_ref.shape, F32)

    def finalize(i):
        rows = pl.ds(pl.multiple_of(i * tq, tq), tq)
        lp = lam_ref[...]
        lam = (jnp.exp(jnp.sum(lp[0:1] * lp[1:2], axis=-1, keepdims=True))
               - jnp.exp(jnp.sum(lp[2:3] * lp[3:4], axis=-1, keepdims=True)) + LAMBDA_INIT)
        for hh in range(hp):
            a1, a2 = acc_ref[2 * hh], acc_ref[2 * hh + 1]
            inv1 = 1.0 / a1[V_HEAD_DIM:V_HEAD_DIM + 1]
            inv2 = 1.0 / a2[V_HEAD_DIM:V_HEAD_DIM + 1]
            ot = a1[:V_HEAD_DIM] * inv1 - lam * (a2[:V_HEAD_DIM] * inv2)
            ot = ot * lax.rsqrt(jnp.mean(ot * ot, axis=0, keepdims=True) + NORM_EPS)
            o_ref[rows, hh * V_HEAD_DIM:(hh + 1) * V_HEAD_DIM] = (ot.T * g_ref[...] * (1.0 - LAMBDA_INIT)).astype(BF16)

    qs0 = masked_queries(0)
    for hh in range(hp):
        for c in range(2):
            scores(0, a_bufs, qs0, hh, c)
    reset_state()
    phase(0, c_bufs, masked_queries(min(1, nq - 1)), 0, a_bufs, masked=True)
    finalize(0)

    def tile(i, tile_carry):
        qs = masked_queries(i)
        qs_next = masked_queries(jnp.minimum(i + 1, nq - 1))
        reset_state()
        phase(1, a_bufs, qs, 0, c_bufs)

        def pair(t, carry):
            phase(2 * t + 2, b_bufs, qs, 2 * t + 1, a_bufs)
            phase(2 * t + 3, a_bufs, qs, 2 * t + 2, b_bufs)
            return carry

        def quad(u, carry):
            pair(2 * u, carry)
            pair(2 * u + 1, carry)
            return carry

        def octet(v, carry):
            quad(2 * v, carry)
            quad(2 * v + 1, carry)
            return carry

        n_pairs = (i - 1) // 2
        n_quads = n_pairs // 2
        lax.fori_loop(0, n_quads // 2, octet, 0)
        lax.fori_loop(2 * (n_quads // 2), n_quads, quad, 0)
        lax.fori_loop(2 * n_quads, n_pairs, pair, 0)

        @pl.when(i % 2 == 0)
        def _():
            phase(i, b_bufs, qs, i - 1, a_bufs)
            phase(0, c_bufs, qs_next, i, b_bufs, masked=True)
            finalize(i)

        @pl.when(i % 2 == 1)
        def _():
            phase(0, c_bufs, qs_next, i, a_bufs, masked=True)
            finalize(i)

        return tile_carry

    lax.fori_loop(1, nq, tile, 0)


def _attention(lam_params, subln, q, k, vt, *, tq=512, hp=2):
    b, s, d = q.shape
    tk = vt.shape[-1]
    assert tq == tk
    w = hp * V_HEAD_DIM
    qo = pl.BlockSpec((None, s, w), lambda bi, g: (bi, 0, g))
    kspec = qo
    vspec = pl.BlockSpec((s // tk, hp, VT_ROWS, tk), lambda bi, g: (bi, g, 0, 0))
    streams = 2 * hp
    return pl.pallas_call(
        functools.partial(_attn_kernel, tq=tq, tk=tk, hp=hp, nq=s // tq),
        grid=(b, N_HEADS // hp),
        in_specs=[_const_spec(lam_params.shape), _const_spec((1, V_HEAD_DIM)), qo, kspec, vspec],
        out_specs=qo,
        out_shape=jax.ShapeDtypeStruct((b, s, d), BF16),
        scratch_shapes=[pltpu.VMEM((streams, 1, tq), F32), pltpu.VMEM((streams, VT_ROWS, tq), F32),
                        pltpu.VMEM((streams, tk, tq), F32), pltpu.VMEM((streams, tk, tq), F32),
                        pltpu.VMEM((streams, tk, tq), F32),
                        pltpu.VMEM((streams, 1, tq), F32), pltpu.VMEM((streams, 1, tq), F32),
                        pltpu.VMEM((streams, 1, tq), F32)],
        compiler_params=pltpu.CompilerParams(dimension_semantics=("arbitrary",) * 2, vmem_limit_bytes=VMEM_LIMIT),
        name="diff_attn",
    )(lam_params, subln, q, k, vt)


def _merge_kernel(x_ref, o_ref, c_ref, halo_ref, gate_ref, cw_ref, cb_ref, lg_ref, lb_ref,
                  wa_ref, wb_ref, wo_ref, out_ref, buf_ref, cv_ref, *, tm, d, tiles_per_seq, rows):
    y_a = jnp.dot(o_ref[...], wa_ref[...], preferred_element_type=F32)
    first = pl.program_id(0) % tiles_per_seq == 0
    halo = halo_ref[...]
    buf_ref[0:CONV_HALO, :] = jnp.where(first, jnp.zeros_like(halo), halo)
    buf_ref[CONV_HALO:, :] = c_ref[...]

    groups = rows // SUBLANES
    for r0 in range(0, tm, rows):
        for lb in range(d // LANES):
            cols = slice(lb * LANES, (lb + 1) * LANES)
            win = buf_ref[r0:r0 + rows + CONV_HALO, cols]
            acc = jnp.broadcast_to(cb_ref[:, cols], (groups, SUBLANES, LANES))
            for r in range(SUBLANES):
                wr = win if r == 0 else pltpu.roll(win, r, 0)
                for a in range(CONV_HALO // SUBLANES):
                    shift = SUBLANES * a + r
                    if shift < CONV_WIDTH:
                        j = CONV_WIDTH - 1 - shift
                        lo = CONV_HALO - SUBLANES * a
                        w8 = jnp.broadcast_to(cw_ref[j:j + 1, cols], (SUBLANES, LANES))
                        acc = acc + w8[None] * wr[lo:lo + rows].reshape(groups, SUBLANES, LANES)
            cv_ref[r0:r0 + rows, cols] = acc.reshape(rows, LANES)

    cv = cv_ref[...]
    mu = jnp.mean(cv, axis=-1, keepdims=True)
    cen = cv - mu
    var = jnp.mean(cen * cen, axis=-1, keepdims=True)
    y = cen * lax.rsqrt(var + NORM_EPS) * lg_ref[...] + lb_ref[...]
    y = (y * jax.nn.sigmoid(y)).astype(BF16)
    y_b = jnp.dot(y, wb_ref[...], preferred_element_type=F32)
    mix = (gate_ref[:, :d] * y_a + gate_ref[:, d:] * y_b).astype(BF16)
    out_ref[...] = x_ref[...] + jnp.dot(mix, wo_ref[...], preferred_element_type=F32)


def _merge(x, o, c, gates, conv_w, conv_b, ln_g, ln_b, wa, wb, wo, *, seq, tm=512, rows=64):
    n, d = x.shape
    row = pl.BlockSpec((tm, d), lambda i: (i, 0))
    halo = pl.BlockSpec((CONV_HALO, d), lambda i: (jnp.maximum(i * (tm // CONV_HALO) - 1, 0), 0))
    vec = _const_spec((1, d))
    mat = _const_spec((d, d))
    return pl.pallas_call(
        functools.partial(_merge_kernel, tm=tm, d=d, tiles_per_seq=seq // tm, rows=rows),
        grid=(n // tm,),
        in_specs=[row, row, row, halo, pl.BlockSpec((tm, 2 * d), lambda i: (i, 0)),
                  _const_spec(conv_w.shape), vec, vec, vec, mat, mat, mat],
        out_specs=row,
        out_shape=jax.ShapeDtypeStruct((n, d), F32),
        scratch_shapes=[pltpu.VMEM((CONV_HALO + tm, d), F32), pltpu.VMEM((tm, d), F32)],
        compiler_params=pltpu.CompilerParams(dimension_semantics=("arbitrary",), vmem_limit_bytes=VMEM_LIMIT),
        name="conv_merge",
    )(x, o, c, c, gates, conv_w, conv_b, ln_g, ln_b, wa, wb, wo)


def _rope_tables(seq):
    pos = jnp.arange(seq, dtype=F32)
    inv_freq = ROPE_THETA ** (-jnp.arange(0, ROT_DIM, 2, dtype=F32) / ROT_DIM)
    ang = pos[:, None] * inv_freq[None, :]
    cos, sin = jnp.cos(ang), jnp.sin(ang)
    half = ROT_DIM // 2
    pad = HEAD_DIM - ROT_DIM
    ones = jnp.ones((seq, pad), F32)
    zeros_h = jnp.zeros((seq, half), F32)
    zeros_p = jnp.zeros((seq, pad), F32)
    ct = jnp.concatenate([cos, cos, ones], axis=-1)
    s1 = jnp.concatenate([-sin, zeros_h, zeros_p], axis=-1)
    s2 = jnp.concatenate([zeros_h, sin, zeros_p], axis=-1)
    rep = LANES // HEAD_DIM
    return tuple(jnp.tile(t, (1, rep)) for t in (ct, s1, s2))


def kernel(x, ffn1_norm, ffn1_w_gate_up, ffn1_w_down, mix_norm, w_in, b_gate, lambda_q1, lambda_k1, lambda_q2, lambda_k2, attn_subln, w_attn_out, conv_w, conv_b, conv_ln_g, conv_ln_b, w_conv_out, w_out, ffn2_norm, ffn2_w_gate_up, ffn2_w_down, final_norm):
    bsz, seq, d = x.shape
    n = bsz * seq
    l = 0
    xf = x.reshape(n, d)
    ct, s1, s2 = _rope_tables(seq)
    final_gain = final_norm.reshape(1, d)

    x1 = _ffn(xf, ffn1_norm[l].reshape(1, d), ffn1_w_gate_up[l].astype(BF16), ffn1_w_down[l].astype(BF16),
              final_gain, final_norm=False)
    q, k, vt, c, gates = _inproj(x1, mix_norm[l].reshape(1, d), w_in[l].astype(BF16), b_gate[l].reshape(1, 2 * d),
                                ct, s1, s2, seq=seq)
    lam_params = jnp.stack([lambda_q1[l], lambda_k1[l], lambda_q2[l], lambda_k2[l]]).astype(F32)
    o = _attention(lam_params, attn_subln[l].reshape(1, V_HEAD_DIM),
                   q.reshape(bsz, seq, d), k.reshape(bsz, seq, d), vt)
    x2 = _merge(x1, o.reshape(n, d), c, gates, conv_w[l], conv_b[l].reshape(1, d),
                conv_ln_g[l].reshape(1, d), conv_ln_b[l].reshape(1, d),
                w_attn_out[l].astype(BF16), w_conv_out[l].astype(BF16), w_out[l].astype(BF16), seq=seq)
    x3 = _ffn(x2, ffn2_norm[l].reshape(1, d), ffn2_w_gate_up[l].astype(BF16), ffn2_w_down[l].astype(BF16),
              final_gain, final_norm=True)
    return x3.reshape(bsz, seq, d)
```

```python
import functools
import math

import jax
import jax.numpy as jnp
from jax import lax
from jax.experimental import pallas as pl
from jax.experimental.pallas import tpu as pltpu

N_HEADS = 8
HEAD_DIM = 64
V_HEAD_DIM = 2 * HEAD_DIM
ROPE_THETA = 500000.0
ROT_DIM = HEAD_DIM // 4
CONV_WIDTH = 31
NORM_EPS = 1e-5
NEG_INF = -1e30
LAMBDA_INIT = 0.8 - 0.6 * math.exp(-0.3 * 0)
Q_SCALE = HEAD_DIM ** -0.5 * math.log2(math.e)

LANES = 128
SUBLANES = 8
BF16_ROWS = 16
VT_ROWS = 2 * 64 + BF16_ROWS
CONV_HALO = 32
VMEM_LIMIT = 56 * 1024 * 1024

BF16 = jnp.bfloat16
F32 = jnp.float32


def _rms(x, gain):
    return x * lax.rsqrt(jnp.mean(x * x, axis=-1, keepdims=True) + NORM_EPS) * gain


def _const_spec(shape):
    return pl.BlockSpec(shape, lambda *_: (0,) * len(shape), pipeline_mode=pl.Buffered(1))


def _ffn_kernel(x_ref, g_ref, wgu_ref, wd_ref, fg_ref, o_ref, *, d_ff, chunk, final_norm):
    x = x_ref[...]
    xn = _rms(x, g_ref[...]).astype(BF16)
    y = jnp.zeros(x.shape, F32)
    for c in range(d_ff // chunk):
        a = jnp.dot(xn, wgu_ref[:, c * chunk:(c + 1) * chunk], preferred_element_type=F32)
        b = jnp.dot(xn, wgu_ref[:, d_ff + c * chunk:d_ff + (c + 1) * chunk], preferred_element_type=F32)
        act = (a * jax.nn.sigmoid(a) * b).astype(BF16)
        y = y + jnp.dot(act, wd_ref[c * chunk:(c + 1) * chunk, :], preferred_element_type=F32)
    out = x + 0.5 * y
    if final_norm:
        out = _rms(out, fg_ref[...])
    o_ref[...] = out


def _ffn(x, gain, wgu, wd, final_gain, *, final_norm, tm=1024, chunk=256):
    n, d = x.shape
    d_ff = wd.shape[0]
    row = pl.BlockSpec((tm, d), lambda i: (i, 0))
    return pl.pallas_call(
        functools.partial(_ffn_kernel, d_ff=d_ff, chunk=chunk, final_norm=final_norm),
        grid=(n // tm,),
        in_specs=[row, _const_spec((1, d)), _const_spec(wgu.shape), _const_spec(wd.shape), _const_spec((1, d))],
        out_specs=row,
        out_shape=jax.ShapeDtypeStruct((n, d), F32),
        compiler_params=pltpu.CompilerParams(dimension_semantics=("arbitrary",), vmem_limit_bytes=VMEM_LIMIT),
        name="ffn_final" if final_norm else "ffn",
    )(x, gain, wgu, wd, final_gain)


def _inproj_kernel(x_ref, g_ref, w_ref, bg_ref, ct_ref, s1_ref, s2_ref,
                   q_ref, k_ref, vt_ref, c_ref, gate_ref, *, d, chunk):
    tm = x_ref.shape[0]
    half = tm // 2
    ones_rows = (lax.broadcasted_iota(jnp.int32, (BF16_ROWS, half), 0) == 0).astype(BF16)
    for hf in range(2):
        rows = slice(hf * half, (hf + 1) * half)
        hn = _rms(x_ref[rows, :], g_ref[...]).astype(BF16)
        ct, s1, s2 = ct_ref[rows, :], s1_ref[rows, :], s2_ref[rows, :]

        def proj(lo):
            return jnp.dot(hn, w_ref[:, lo:lo + chunk], preferred_element_type=F32)

        def rope(z):
            blocks = []
            for b in range(chunk // LANES):
                zb = z[:, b * LANES:(b + 1) * LANES]
                up = pltpu.roll(zb, LANES - ROT_DIM // 2, 1)
                dn = pltpu.roll(zb, ROT_DIM // 2, 1)
                blocks.append(zb * ct + up * s1 + dn * s2)
            return jnp.concatenate(blocks, axis=-1)

        for c in range(d // chunk):
            lo = c * chunk
            q_ref[rows, lo:lo + chunk] = (rope(proj(lo)) * Q_SCALE).astype(BF16)
            k_ref[rows, lo:lo + chunk] = rope(proj(d + lo)).astype(BF16)
            vz = proj(2 * d + lo)
            for b in range(chunk // V_HEAD_DIM):
                h = (lo + b * V_HEAD_DIM) // V_HEAD_DIM
                vt_ref[0, h, :V_HEAD_DIM, rows] = vz[:, b * V_HEAD_DIM:(b + 1) * V_HEAD_DIM].T.astype(BF16)
                vt_ref[0, h, V_HEAD_DIM:, rows] = ones_rows
            c_ref[rows, lo:lo + chunk] = proj(3 * d + lo) * jax.nn.sigmoid(proj(4 * d + lo))
        for c in range(2 * d // chunk):
            lo = c * chunk
            gate_ref[rows, lo:lo + chunk] = jax.nn.sigmoid(proj(5 * d + lo) + bg_ref[:, lo:lo + chunk])


def _inproj(x, gain, w_in, b_gate, ct, s1, s2, *, seq, tm=512, chunk=512):
    n, d = x.shape
    row = pl.BlockSpec((tm, d), lambda i: (i, 0))
    tab = pl.BlockSpec((tm, LANES), lambda i: (i % (seq // tm), 0))
    return pl.pallas_call(
        functools.partial(_inproj_kernel, d=d, chunk=chunk),
        grid=(n // tm,),
        in_specs=[row, _const_spec((1, d)), _const_spec(w_in.shape), _const_spec((1, 2 * d)), tab, tab, tab],
        out_specs=[row, row, pl.BlockSpec((1, N_HEADS, VT_ROWS, tm), lambda i: (i, 0, 0, 0)),
                   row, pl.BlockSpec((tm, 2 * d), lambda i: (i, 0))],
        out_shape=[jax.ShapeDtypeStruct((n, d), BF16)] * 2
        + [jax.ShapeDtypeStruct((n // tm, N_HEADS, VT_ROWS, tm), BF16),
           jax.ShapeDtypeStruct((n, d), F32), jax.ShapeDtypeStruct((n, 2 * d), F32)],
        compiler_params=pltpu.CompilerParams(dimension_semantics=("arbitrary",), vmem_limit_bytes=VMEM_LIMIT),
        name="inproj",
    )(x, gain, w_in, b_gate, ct, s1, s2)


def _attn_kernel(lam_ref, g_ref, q_ref, k_ref, vt_ref, o_ref,
                 m_ref, acc_ref, sa_ref, sb_ref, sc_ref, xa_ref, xb_ref, xc_ref, *, tq, tk, hp, nq):
    a_bufs, b_bufs, c_bufs = (sa_ref, xa_ref), (sb_ref, xb_ref), (sc_ref, xc_ref)
    lane = lax.broadcasted_iota(jnp.int32, (tq, V_HEAD_DIM), 1)

    def masked_queries(i):
        rows = pl.ds(pl.multiple_of(i * tq, tq), tq)
        qs = []
        for hh in range(hp):
            q = q_ref[rows, hh * V_HEAD_DIM:(hh + 1) * V_HEAD_DIM]
            qs.append((jnp.where(lane < HEAD_DIM, q, jnp.zeros_like(q)), jnp.where(lane >= HEAD_DIM, q, jnp.zeros_like(q))))
        return qs

    def scores(j, bufs, qs, hh, c):
        kc = k_ref[pl.ds(pl.multiple_of(j * tk, tk), tk), hh * V_HEAD_DIM:(hh + 1) * V_HEAD_DIM]
        s = lax.dot_general(kc, qs[hh][c], (((1,), (1,)), ((), ())), preferred_element_type=F32)
        bufs[0][2 * hh + c] = s
        bufs[1][2 * hh + c] = jnp.max(s, axis=0, keepdims=True)

    def consume(j, bufs, hh, c, masked):
        st = 2 * hh + c
        vt = vt_ref[j, hh]
        s = bufs[0][st]
        if masked:
            key = lax.broadcasted_iota(jnp.int32, s.shape, 0)
            qry = lax.broadcasted_iota(jnp.int32, s.shape, 1)
            s = jnp.where(key <= qry, s, NEG_INF)
            blk_max = jnp.max(s, axis=0, keepdims=True)
        else:
            blk_max = bufs[1][st]
        m_prev = m_ref[st]
        m_new = jnp.maximum(m_prev, blk_max)
        p = jnp.exp2(s - m_new).astype(BF16)
        alpha = jnp.exp2(m_prev - m_new)
        acc_ref[st] = alpha * acc_ref[st] + jnp.dot(vt, p, preferred_element_type=F32)
        m_ref[st] = m_new

    def phase(j_next, nxt, qs_next, j_cur, cur, masked=False):
        for hh in range(hp):
            for c in range(2):
                scores(j_next, nxt, qs_next, hh, c)
                consume(j_cur, cur, hh, c, masked)

    def reset_state():
        m_ref[...] = jnp.full(m_ref.shape, NEG_INF, F32)
        acc_ref[...] = jnp.zeros(acc_ref.shape, F32)

    def finalize(i):
        rows = pl.ds(pl.multiple_of(i * tq, tq), tq)
        lp = lam_ref[...]
        lam = (jnp.exp(jnp.sum(lp[0:1] * lp[1:2], axis=-1, keepdims=True))
               - jnp.exp(jnp.sum(lp[2:3] * lp[3:4], axis=-1, keepdims=True)) + LAMBDA_INIT)
        for hh in range(hp):
            a1, a2 = acc_ref[2 * hh], acc_ref[2 * hh + 1]
            inv1 = 1.0 / a1[V_HEAD_DIM:V_HEAD_DIM + 1]
            inv2 = 1.0 / a2[V_HEAD_DIM:V_HEAD_DIM + 1]
            ot = a1[:V_HEAD_DIM] * inv1 - lam * (a2[:V_HEAD_DIM] * inv2)
            ot = ot * lax.rsqrt(jnp.mean(ot * ot, axis=0, keepdims=True) + NORM_EPS)
            o_ref[rows, hh * V_HEAD_DIM:(hh + 1) * V_HEAD_DIM] = (ot.T * g_ref[...] * (1.0 - LAMBDA_INIT)).astype(BF16)

    qs0 = masked_queries(0)
    for hh in range(hp):
        for c in range(2):
            scores(0, a_bufs, qs0, hh, c)
    reset_state()
    phase(0, c_bufs, masked_queries(min(1, nq - 1)), 0, a_bufs, masked=True)
    finalize(0)

    def tile(i, tile_carry):
        qs = masked_queries(i)
        qs_next = masked_queries(jnp.minimum(i + 1, nq - 1))
        reset_state()
        phase(1, a_bufs, qs, 0, c_bufs)

        def pair(t, carry):
            phase(2 * t + 2, b_bufs, qs, 2 * t + 1, a_bufs)
            phase(2 * t + 3, a_bufs, qs, 2 * t + 2, b_bufs)
            return carry

        def quad(u, carry):
            pair(2 * u, carry)
            pair(2 * u + 1, carry)
            return carry

        def octet(v, carry):
            quad(2 * v, carry)
            quad(2 * v + 1, carry)
            return carry

        n_pairs = (i - 1) // 2
        n_quads = n_pairs // 2
        lax.fori_loop(0, n_quads // 2, octet, 0)
        lax.fori_loop(2 * (n_quads // 2), n_quads, quad, 0)
        lax.fori_loop(2 * n_quads, n_pairs, pair, 0)

        @pl.when(i % 2 == 0)
        def _():
            phase(i, b_bufs, qs, i - 1, a_bufs)
            phase(0, c_bufs, qs_next, i, b_bufs, masked=True)
            finalize(i)

        @pl.when(i % 2 == 1)
        def _():
            phase(0, c_bufs, qs_next, i, a_bufs, masked=True)
            finalize(i)

        return tile_carry

    lax.fori_loop(1, nq, tile, 0)


def _attention(lam_params, subln, q, k, vt, *, tq=512, hp=2):
    b, s, d = q.shape
    tk = vt.shape[-1]
    assert tq == tk
    w = hp * V_HEAD_DIM
    qo = pl.BlockSpec((None, s, w), lambda bi, g: (bi, 0, g))
    kspec = qo
    vspec = pl.BlockSpec((s // tk, hp, VT_ROWS, tk), lambda bi, g: (bi, g, 0, 0))
    streams = 2 * hp
    return pl.pallas_call(
        functools.partial(_attn_kernel, tq=tq, tk=tk, hp=hp, nq=s // tq),
        grid=(b, N_HEADS // hp),
        in_specs=[_const_spec(lam_params.shape), _const_spec((1, V_HEAD_DIM)), qo, kspec, vspec],
        out_specs=qo,
        out_shape=jax.ShapeDtypeStruct((b, s, d), BF16),
        scratch_shapes=[pltpu.VMEM((streams, 1, tq), F32), pltpu.VMEM((streams, VT_ROWS, tq), F32),
                        pltpu.VMEM((streams, tk, tq), F32), pltpu.VMEM((streams, tk, tq), F32),
                        pltpu.VMEM((streams, tk, tq), F32),
                        pltpu.VMEM((streams, 1, tq), F32), pltpu.VMEM((streams, 1, tq), F32),
                        pltpu.VMEM((streams, 1, tq), F32)],
        compiler_params=pltpu.CompilerParams(dimension_semantics=("arbitrary",) * 2, vmem_limit_bytes=VMEM_LIMIT),
        name="diff_attn",
    )(lam_params, subln, q, k, vt)


def _merge_kernel(x_ref, o_ref, c_ref, halo_ref, gate_ref, cw_ref, cb_ref, lg_ref, lb_ref,
                  wa_ref, wb_ref, wo_ref, out_ref, buf_ref, cv_ref, *, tm, d, tiles_per_seq, rows):
    y_a = jnp.dot(o_ref[...], wa_ref[...], preferred_element_type=F32)
    first = pl.program_id(0) % tiles_per_seq == 0
    halo = halo_ref[...]
    buf_ref[0:CONV_HALO, :] = jnp.where(first, jnp.zeros_like(halo), halo)
    buf_ref[CONV_HALO:, :] = c_ref[...]

    groups = rows // SUBLANES
    for r0 in range(0, tm, rows):
        for lb in range(d // LANES):
            cols = slice(lb * LANES, (lb + 1) * LANES)
            win = buf_ref[r0:r0 + rows + CONV_HALO, cols]
            acc = jnp.broadcast_to(cb_ref[:, cols], (groups, SUBLANES, LANES))
            for r in range(SUBLANES):
                wr = win if r == 0 else pltpu.roll(win, r, 0)
                for a in range(CONV_HALO // SUBLANES):
                    shift = SUBLANES * a + r
                    if shift < CONV_WIDTH:
                        j = CONV_WIDTH - 1 - shift
                        lo = CONV_HALO - SUBLANES * a
                        w8 = jnp.broadcast_to(cw_ref[j:j + 1, cols], (SUBLANES, LANES))
                        acc = acc + w8[None] * wr[lo:lo + rows].reshape(groups, SUBLANES, LANES)
            cv_ref[r0:r0 + rows, cols] = acc.reshape(rows, LANES)

    cv = cv_ref[...]
    mu = jnp.mean(cv, axis=-1, keepdims=True)
    cen = cv - mu
    var = jnp.mean(cen * cen, axis=-1, keepdims=True)
    y = cen * lax.rsqrt(var + NORM_EPS) * lg_ref[...] + lb_ref[...]
    y = (y * jax.nn.sigmoid(y)).astype(BF16)
    y_b = jnp.dot(y, wb_ref[...], preferred_element_type=F32)
    mix = (gate_ref[:, :d] * y_a + gate_ref[:, d:] * y_b).astype(BF16)
    out_ref[...] = x_ref[...] + jnp.dot(mix, wo_ref[...], preferred_element_type=F32)


def _merge(x, o, c, gates, conv_w, conv_b, ln_g, ln_b, wa, wb, wo, *, seq, tm=512, rows=64):
    n, d = x.shape
    row = pl.BlockSpec((tm, d), lambda i: (i, 0))
    halo = pl.BlockSpec((CONV_HALO, d), lambda i: (jnp.maximum(i * (tm // CONV_HALO) - 1, 0), 0))
    vec = _const_spec((1, d))
    mat = _const_spec((d, d))
    return pl.pallas_call(
        functools.partial(_merge_kernel, tm=tm, d=d, tiles_per_seq=seq // tm, rows=rows),
        grid=(n // tm,),
        in_specs=[row, row, row, halo, pl.BlockSpec((tm, 2 * d), lambda i: (i, 0)),
                  _const_spec(conv_w.shape), vec, vec, vec, mat, mat, mat],
        out_specs=row,
        out_shape=jax.ShapeDtypeStruct((n, d), F32),
        scratch_shapes=[pltpu.VMEM((CONV_HALO + tm, d), F32), pltpu.VMEM((tm, d), F32)],
        compiler_params=pltpu.CompilerParams(dimension_semantics=("arbitrary",), vmem_limit_bytes=VMEM_LIMIT),
        name="conv_merge",
    )(x, o, c, c, gates, conv_w, conv_b, ln_g, ln_b, wa, wb, wo)


def _rope_tables(seq):
    pos = jnp.arange(seq, dtype=F32)
    inv_freq = ROPE_THETA ** (-jnp.arange(0, ROT_DIM, 2, dtype=F32) / ROT_DIM)
    ang = pos[:, None] * inv_freq[None, :]
    cos, sin = jnp.cos(ang), jnp.sin(ang)
    half = ROT_DIM // 2
    pad = HEAD_DIM - ROT_DIM
    ones = jnp.ones((seq, pad), F32)
    zeros_h = jnp.zeros((seq, half), F32)
    zeros_p = jnp.zeros((seq, pad), F32)
    ct = jnp.concatenate([cos, cos, ones], axis=-1)
    s1 = jnp.concatenate([-sin, zeros_h, zeros_p], axis=-1)
    s2 = jnp.concatenate([zeros_h, sin, zeros_p], axis=-1)
    rep = LANES // HEAD_DIM
    return tuple(jnp.tile(t, (1, rep)) for t in (ct, s1, s2))


def kernel(x, ffn1_norm, ffn1_w_gate_up, ffn1_w_down, mix_norm, w_in, b_gate, lambda_q1, lambda_k1, lambda_q2, lambda_k2, attn_subln, w_attn_out, conv_w, conv_b, conv_ln_g, conv_ln_b, w_conv_out, w_out, ffn2_norm, ffn2_w_gate_up, ffn2_w_down, final_norm):
    bsz, seq, d = x.shape
    n = bsz * seq
    l = 0
    xf = x.reshape(n, d)
    ct, s1, s2 = _rope_tables(seq)
    final_gain = final_norm.reshape(1, d)

    x1 = _ffn(xf, ffn1_norm[l].reshape(1, d), ffn1_w_gate_up[l].astype(BF16), ffn1_w_down[l].astype(BF16),
              final_gain, final_norm=False)
    q, k, vt, c, gates = _inproj(x1, mix_norm[l].reshape(1, d), w_in[l].astype(BF16), b_gate[l].reshape(1, 2 * d),
                                ct, s1, s2, seq=seq)
    lam_params = jnp.stack([lambda_q1[l], lambda_k1[l], lambda_q2[l], lambda_k2[l]]).astype(F32)
    o = _attention(lam_params, attn_subln[l].reshape(1, V_HEAD_DIM),
                   q.reshape(bsz, seq, d), k.reshape(bsz, seq, d), vt)
    x2 = _merge(x1, o.reshape(n, d), c, gates, conv_w[l], conv_b[l].reshape(1, d),
                conv_ln_g[l].reshape(1, d), conv_ln_b[l].reshape(1, d),
                w_attn_out[l].astype(BF16), w_conv_out[l].astype(BF16), w_out[l].astype(BF16), seq=seq)
    x3 = _ffn(x2, ffn2_norm[l].reshape(1, d), ffn2_w_gate_up[l].astype(BF16), ffn2_w_down[l].astype(BF16),
              final_gain, final_norm=True)
    return x3.reshape(bsz, seq, d)
```
